```python
import jax, jax.numpy as jnp
from jax import lax
import numpy as np

D_MODEL = 1024
BATCH = 8
SEQ = 8192
DEPTH = 1
DEC_BATCH = 2
DEC_SEQ = 8192
PAST_LEN = 128

GRID_W = 64
N_HEADS = 8
HEAD_DIM = 64
D_ATTN = N_HEADS * HEAD_DIM
D_CONV = 512
CONV_W = 3
WIN_ROWS_MAX = 8
WIN_COLS = 16
COL_BLOCK = 16
KEY_COL_BLOCK = COL_BLOCK + WIN_COLS
N_COL_BLOCKS = GRID_W // COL_BLOCK
N_BRANCHES = 2
D_IN_PROJ = 3 * D_ATTN + 3 * D_CONV + N_BRANCHES * D_MODEL
D_FF = ((8 * D_MODEL + 3 * 256 - 1) // (3 * 256)) * 256
RMS_EPS = 1e-6
NEG_INF = -1e30

kernel_name = "hybrid_natten_shortconv_encoder"


def rms_norm(x, g):
    xf = x.astype(jnp.float32)
    y = xf * lax.rsqrt(jnp.mean(xf * xf, axis=-1, keepdims=True) + RMS_EPS)
    return (y * g.astype(jnp.float32)).astype(x.dtype)


def neighbourhood_attention(q, k, v, rpb):
    bsz, seq_len, _ = q.shape
    rows = seq_len // GRID_W
    wr = min(WIN_ROWS_MAX, rows)
    qg = q.reshape(bsz, rows, N_COL_BLOCKS, COL_BLOCK, N_HEADS, HEAD_DIM)
    kg = k.reshape(bsz, rows, GRID_W, N_HEADS, HEAD_DIM)
    vg = v.reshape(bsz, rows, GRID_W, N_HEADS, HEAD_DIM)

    j = np.arange(N_COL_BLOCKS)
    kcol = np.clip(j * COL_BLOCK - WIN_COLS // 2, 0, GRID_W - KEY_COL_BLOCK)[:, None] + np.arange(KEY_COL_BLOCK)[None, :]
    qcol = j[:, None] * COL_BLOCK + np.arange(COL_BLOCK)[None, :]
    cstart = np.clip(qcol - WIN_COLS // 2, 0, GRID_W - WIN_COLS)
    valid = (kcol[:, None, :] >= cstart[:, :, None]) & (kcol[:, None, :] < cstart[:, :, None] + WIN_COLS)
    col_idx = np.clip(kcol[:, None, :] - qcol[:, :, None] + WIN_COLS - 1, 0, 2 * WIN_COLS - 2)
    valid = jnp.asarray(valid)
    rpb_cols = rpb[:, :, col_idx]
    scale = HEAD_DIM ** -0.5

    def row_step(r):
        rs = jnp.clip(r - wr // 2, 0, rows - wr)
        qb = lax.dynamic_index_in_dim(qg, r, axis=1, keepdims=False)
        kr = lax.dynamic_slice_in_dim(kg, rs, wr, axis=1)
        vr = lax.dynamic_slice_in_dim(vg, rs, wr, axis=1)
        kb = kr[:, :, kcol]
        vb = vr[:, :, kcol]
        s = jnp.einsum('bjqhd,bwjkhd->bhjqwk', qb, kb).astype(jnp.float32) * scale
        ri = rs + jnp.arange(wr) - r + (WIN_ROWS_MAX - 1)
        bias = jnp.take(rpb_cols, ri, axis=1).transpose(0, 2, 3, 1, 4)
        s = s + bias[None].astype(jnp.float32)
        s = jnp.where(valid[None, None, :, :, None, :], s, NEG_INF)
        sh = s.shape
        p = jax.nn.softmax(s.reshape(sh[:4] + (wr * KEY_COL_BLOCK,)), axis=-1).reshape(sh)
        return jnp.einsum('bhjqwk,bwjkhd->bjqhd', p.astype(v.dtype), vb)

    out = lax.map(row_step, jnp.arange(rows))
    return out.transpose(1, 0, 2, 3, 4, 5).reshape(bsz, seq_len, D_ATTN)


def short_conv(z, w):
    zp = jnp.pad(z, ((0, 0), (1, 1), (0, 0)))
    return zp[:, :-2] * w[0] + zp[:, 1:-1] * w[1] + zp[:, 2:] * w[2]


def mixer(xn, w_in, b_gate, rpb, conv_w, w_attn_branch, w_conv_branch, w_out):
    proj = jnp.einsum('bld,de->ble', xn, w_in)
    splits = np.cumsum([D_ATTN, D_ATTN, D_ATTN, D_CONV, D_CONV, D_CONV])
    q, k, v, u, bg, cg, gl = jnp.split(proj, splits, axis=-1)
    a = neighbourhood_attention(q, k, v, rpb)
    c = bg * short_conv(cg * u, conv_w)
    gates = jax.nn.sigmoid(gl + b_gate)
    g_a, g_c = jnp.split(gates, 2, axis=-1)
    merged = g_a * jnp.einsum('ble,ed->bld', a, w_attn_branch) + g_c * jnp.einsum('ble,ed->bld', c, w_conv_branch)
    return jnp.einsum('bld,de->ble', merged, w_out)


def swiglu(xn, w_ffn_in, w_ffn_down):
    h = jnp.einsum('bld,df->blf', xn, w_ffn_in)
    gate, up = jnp.split(h, 2, axis=-1)
    return jnp.einsum('blf,fd->bld', jax.nn.silu(gate) * up, w_ffn_down)


def trunk(x, norm_mix_g, w_in, b_gate, rpb, conv_w, w_attn_branch, w_conv_branch, w_out,
          norm_ffn_g, w_ffn_in, w_ffn_down, norm_final_g):
    for i in range(DEPTH):
        x = x + mixer(rms_norm(x, norm_mix_g[i]), w_in[i], b_gate[i], rpb[i], conv_w[i],
                      w_attn_branch[i], w_conv_branch[i], w_out[i])
        x = x + swiglu(rms_norm(x, norm_ffn_g[i]), w_ffn_in[i], w_ffn_down[i])
    return rms_norm(x, norm_final_g)


def setup_inputs(seed: int = 0) -> dict:
    key = jax.random.key(seed)
    ks = jax.random.split(key, 16)
    f32 = jnp.float32
    n = lambda k, shape, s: jax.random.normal(k, shape, f32) * s
    return {
        "x_prompt": n(ks[0], (BATCH, SEQ, D_MODEL), 1.0),
        "x_sample": n(ks[1], (DEC_BATCH, DEC_SEQ, D_MODEL), 1.0),
        "norm_mix_g": 1.0 + n(ks[2], (DEPTH, D_MODEL), 0.02),
        "w_in": n(ks[3], (DEPTH, D_MODEL, D_IN_PROJ), D_MODEL ** -0.5),
        "b_gate": n(ks[4], (DEPTH, N_BRANCHES * D_MODEL), 0.02),
        "rpb": n(ks[5], (DEPTH, N_HEADS, 2 * WIN_ROWS_MAX - 1, 2 * WIN_COLS - 1), 0.5),
        "conv_w": n(ks[6], (DEPTH, CONV_W, D_CONV), CONV_W ** -0.5),
        "w_attn_branch": n(ks[7], (DEPTH, D_ATTN, D_MODEL), D_ATTN ** -0.5),
        "w_conv_branch": n(ks[8], (DEPTH, D_CONV, D_MODEL), D_CONV ** -0.5),
        "w_out": n(ks[9], (DEPTH, D_MODEL, D_MODEL), D_MODEL ** -0.5),
        "norm_ffn_g": 1.0 + n(ks[10], (DEPTH, D_MODEL), 0.02),
        "w_ffn_in": n(ks[11], (DEPTH, D_MODEL, 2 * D_FF), D_MODEL ** -0.5),
        "w_ffn_down": n(ks[12], (DEPTH, D_FF, D_MODEL), D_FF ** -0.5),
        "norm_final_g": 1.0 + n(ks[13], (D_MODEL,), 0.02),
    }


def reference(x_prompt, x_sample, norm_mix_g, w_in, b_gate, rpb, conv_w, w_attn_branch,
              w_conv_branch, w_out, norm_ffn_g, w_ffn_in, w_ffn_down, norm_final_g):
    y_prompt = trunk(x_prompt, norm_mix_g, w_in, b_gate, rpb, conv_w, w_attn_branch, w_conv_branch,
                     w_out, norm_ffn_g, w_ffn_in, w_ffn_down, norm_final_g)
    y_sample = trunk(x_sample, norm_mix_g, w_in, b_gate, rpb, conv_w, w_attn_branch, w_conv_branch,
                     w_out, norm_ffn_g, w_ffn_in, w_ffn_down, norm_final_g)
    return (y_prompt, y_sample)
```

```python
import functools

import numpy as np
import jax
import jax.numpy as jnp
from jax import lax
from jax.experimental import pallas as pl
from jax.experimental.pallas import tpu as pltpu

D_MODEL = 1024
GRID_W = 64
N_HEADS = 8
HEAD_DIM = 64
D_ATTN = N_HEADS * HEAD_DIM
D_CONV = 512
WIN_ROWS = 8
WIN_COLS = 16
D_FF = 2816
RMS_EPS = 1e-6
NEG_INF = -1e30
ATTN_SCALE = HEAD_DIM ** -0.5

LANES = 128
BF16_SUBLANES = 16
WIN_TOKENS = WIN_ROWS * GRID_W

TM_PROJ = 512
ROWS_ATTN = 16
TM_POST = 512
FF_CHUNK = 256

BF16 = jnp.bfloat16
F32 = jnp.float32


def _const_spec(shape):
    return pl.BlockSpec(shape, lambda *_: (0,) * len(shape), pipeline_mode=pl.Buffered(1))


def _rms_norm(x, g):
    return x * lax.rsqrt(jnp.mean(x * x, axis=-1, keepdims=True) + RMS_EPS) * g


def _inproj_kernel(x_ref, g_ref, w_ref, q_ref, k_ref, v_ref, z_ref, bg_ref, gl_ref):
    xn = _rms_norm(x_ref[0], g_ref[...]).astype(BF16)

    def proj(lo, hi):
        return jnp.dot(xn, w_ref[:, lo:hi], preferred_element_type=F32)

    q_ref[0] = (proj(0, D_ATTN) * ATTN_SCALE).astype(BF16)
    k_ref[0] = proj(D_ATTN, 2 * D_ATTN).astype(BF16)
    v_ref[0] = proj(2 * D_ATTN, 3 * D_ATTN).astype(BF16)
    o = 3 * D_ATTN
    u = proj(o, o + D_CONV)
    bg_ref[0] = proj(o + D_CONV, o + 2 * D_CONV).astype(BF16)
    cg = proj(o + 2 * D_CONV, o + 3 * D_CONV)
    z_ref[0] = (cg * u).astype(BF16)
    o += 3 * D_CONV
    for c in range(0, 2 * D_MODEL, 1024):
        gl_ref[0, :, c:c + 1024] = proj(o + c, o + c + 1024).astype(BF16)


def _inproj(x, g, w_in):
    b, l, _ = x.shape
    d_in = w_in.shape[1]
    tok = lambda width: pl.BlockSpec((1, TM_PROJ, width), lambda bi, i: (bi, i, 0))
    out_widths = (D_ATTN, D_ATTN, D_ATTN, D_CONV, D_CONV, 2 * D_MODEL)
    return pl.pallas_call(
        _inproj_kernel,
        grid=(b, l // TM_PROJ),
        in_specs=[tok(D_MODEL), _const_spec((1, D_MODEL)), _const_spec((D_MODEL, d_in))],
        out_specs=[tok(w) for w in out_widths],
        out_shape=[jax.ShapeDtypeStruct((b, l, w), BF16) for w in out_widths],
        compiler_params=pltpu.CompilerParams(
            dimension_semantics=("parallel", "parallel"), vmem_limit_bytes=48 * 2**20),
        name="inproj",
    )(x, g, w_in)


def _attn_kernel(q_ref, k_ref, v_ref, bias_ref, o_ref, *, rows):
    row0 = pl.program_id(1) * ROWS_ATTN
    lane = lax.broadcasted_iota(jnp.int32, (GRID_W, LANES), 1)
    low_half = lane < HEAD_DIM

    def row_body(i, carry):
        r = row0 + i
        rs = jnp.clip(r - WIN_ROWS // 2, 0, rows - WIN_ROWS)
        delta = r - rs
        q_off = pl.multiple_of(i * GRID_W, GRID_W)
        k_off = pl.multiple_of(rs * GRID_W, GRID_W)
        for hp in range(N_HEADS // 2):
            cols = slice(hp * LANES, (hp + 1) * LANES)
            q_pair = q_ref[0, pl.ds(q_off, GRID_W), cols]
            k_pair = k_ref[0, pl.ds(k_off, WIN_TOKENS), cols]
            v_pair = v_ref[0, pl.ds(k_off, WIN_TOKENS), cols]
            outs = []
            for sub in range(2):
                keep = low_half if sub == 0 else jnp.logical_not(low_half)
                q_h = jnp.where(keep, q_pair, jnp.zeros_like(q_pair))
                s = lax.dot_general(q_h, k_pair, (((1,), (1,)), ((), ())),
                                    preferred_element_type=F32)
                s = s + bias_ref[delta, 2 * hp + sub]
                m = jnp.max(s, axis=-1, keepdims=True)
                p = jnp.exp(s - m)
                l = jnp.sum(p, axis=-1, keepdims=True)
                o = jnp.dot(p.astype(BF16), v_pair, preferred_element_type=F32)
                outs.append(o / l)
            o_ref[0, pl.ds(q_off, GRID_W), cols] = jnp.where(low_half, outs[0], outs[1]).astype(BF16)
        return carry

    lax.fori_loop(0, ROWS_ATTN, row_body, 0)


def _attention(q, k, v, bias):
    b, l, _ = q.shape
    rows = l // GRID_W
    tm = ROWS_ATTN * GRID_W
    tile = pl.BlockSpec((1, tm, D_ATTN), lambda bi, i: (bi, i, 0))
    whole = pl.BlockSpec((1, l, D_ATTN), lambda bi, i: (bi, 0, 0), pipeline_mode=pl.Buffered(1))
    return pl.pallas_call(
        functools.partial(_attn_kernel, rows=rows),
        grid=(b, rows // ROWS_ATTN),
        in_specs=[tile, whole, whole, _const_spec(bias.shape)],
        out_specs=tile,
        out_shape=jax.ShapeDtypeStruct((b, l, D_ATTN), BF16),
        compiler_params=pltpu.CompilerParams(
            dimension_semantics=("parallel", "arbitrary"), vmem_limit_bytes=48 * 2**20),
        name="natten",
    )(q, k, v, bias)


def _bias_table(rpb):
    qcol = np.arange(GRID_W)[:, None]
    kcol = np.arange(GRID_W)[None, :]
    cstart = np.clip(qcol - WIN_COLS // 2, 0, GRID_W - WIN_COLS)
    valid = (kcol >= cstart) & (kcol < cstart + WIN_COLS)
    cidx = np.clip(kcol - qcol + WIN_COLS - 1, 0, 2 * WIN_COLS - 2)
    t = jnp.where(valid[None, None], rpb[:, :, cidx].astype(F32), NEG_INF)
    tabs = []
    for delta in range(WIN_ROWS):
        sl = t[:, WIN_ROWS - 1 - delta: 2 * WIN_ROWS - 1 - delta]
        tabs.append(sl.transpose(0, 2, 1, 3).reshape(N_HEADS, GRID_W, WIN_TOKENS))
    return jnp.stack(tabs)


def _post_kernel(x_ref, a_ref, z_ref, zp_ref, zn_ref, bg_ref, gl_ref, bgate_ref, convw_ref,
                 wa_ref, wc_ref, wo_ref, g2_ref, wg_ref, wu_ref, wd_ref, g3_ref, y_ref):
    i = pl.program_id(1)
    n = pl.num_programs(1)
    tm = z_ref.shape[1]

    z = z_ref[0].astype(F32)
    prev_row = jnp.where(i > 0, zp_ref[0, BF16_SUBLANES - 1:BF16_SUBLANES, :].astype(F32), 0.0)
    next_row = jnp.where(i < n - 1, zn_ref[0, 0:1, :].astype(F32), 0.0)
    ridx = lax.broadcasted_iota(jnp.int32, z.shape, 0)
    z_m1 = jnp.where(ridx == 0, prev_row, pltpu.roll(z, 1, 0))
    z_p1 = jnp.where(ridx == tm - 1, next_row, pltpu.roll(z, tm - 1, 0))
    cw = convw_ref[...]
    c = bg_ref[0].astype(F32) * (z_m1 * cw[0:1] + z * cw[1:2] + z_p1 * cw[2:3])

    gl = gl_ref[0].astype(F32) + bgate_ref[...]
    gates = jax.nn.sigmoid(gl)
    pa = jnp.dot(a_ref[0], wa_ref[...], preferred_element_type=F32)
    pc = jnp.dot(c.astype(BF16), wc_ref[...], preferred_element_type=F32)
    merged = gates[:, :D_MODEL] * pa + gates[:, D_MODEL:] * pc
    x1 = x_ref[0] + jnp.dot(merged.astype(BF16), wo_ref[...], preferred_element_type=F32)

    xn = _rms_norm(x1, g2_ref[...]).astype(BF16)
    acc = x1
    for f in range(0, D_FF, FF_CHUNK):
        gate = jnp.dot(xn, wg_ref[:, f:f + FF_CHUNK], preferred_element_type=F32)
        up = jnp.dot(xn, wu_ref[:, f:f + FF_CHUNK], preferred_element_type=F32)
        act = (gate * jax.nn.sigmoid(gate) * up).astype(BF16)
        acc = acc + jnp.dot(act, wd_ref[f:f + FF_CHUNK, :], preferred_element_type=F32)
    y_ref[0] = _rms_norm(acc, g3_ref[...])


def _post(x, a, z, bg, gl, b_gate, conv_w, wa, wc, wo, g2, wg, wu, wd, g3):
    b, l, _ = x.shape
    halo = TM_POST // BF16_SUBLANES
    n_halo = l // BF16_SUBLANES
    tok = lambda width: pl.BlockSpec((1, TM_POST, width), lambda bi, i: (bi, i, 0))
    z_prev = pl.BlockSpec((1, BF16_SUBLANES, D_CONV),
                          lambda bi, i: (bi, jnp.maximum(i * halo - 1, 0), 0))
    z_next = pl.BlockSpec((1, BF16_SUBLANES, D_CONV),
                          lambda bi, i: (bi, jnp.minimum((i + 1) * halo, n_halo - 1), 0))
    consts = (b_gate, conv_w, wa, wc, wo, g2, wg, wu, wd, g3)
    return pl.pallas_call(
        _post_kernel,
        grid=(b, l // TM_POST),
        in_specs=[tok(D_MODEL), tok(D_ATTN), tok(D_CONV), z_prev, z_next, tok(D_CONV),
                  tok(2 * D_MODEL)] + [_const_spec(c.shape) for c in consts],
        out_specs=tok(D_MODEL),
        out_shape=jax.ShapeDtypeStruct((b, l, D_MODEL), F32),
        compiler_params=pltpu.CompilerParams(
            dimension_semantics=("parallel", "parallel"), vmem_limit_bytes=56 * 2**20),
        name="post",
    )(x, a, z, z, z, bg, gl, *consts)


def kernel(x_prompt, x_sample, norm_mix_g, w_in, b_gate, rpb, conv_w, w_attn_branch, w_conv_branch,
           w_out, norm_ffn_g, w_ffn_in, w_ffn_down, norm_final_g):
    assert w_in.shape[0] == 1, "the final norm is fused into the single layer's last kernel"
    bias = _bias_table(rpb[0])
    w_in_b = w_in[0].astype(BF16)
    wa_b = w_attn_branch[0].astype(BF16)
    wc_b = w_conv_branch[0].astype(BF16)
    wo_b = w_out[0].astype(BF16)
    wg_b = w_ffn_in[0, :, :D_FF].astype(BF16)
    wu_b = w_ffn_in[0, :, D_FF:].astype(BF16)
    wd_b = w_ffn_down[0].astype(BF16)
    g1 = norm_mix_g[0].reshape(1, D_MODEL)
    g2 = norm_ffn_g[0].reshape(1, D_MODEL)
    g3 = norm_final_g.reshape(1, D_MODEL)
    bgate = b_gate[0].reshape(1, 2 * D_MODEL)

    def trunk(x):
        q, k, v, z, bg, gl = _inproj(x, g1, w_in_b)
        a = _attention(q, k, v, bias)
        return _post(x, a, z, bg, gl, bgate, conv_w[0], wa_b, wc_b, wo_b, g2, wg_b, wu_b, wd_b, g3)

    return trunk(x_prompt), trunk(x_sample)
```

```python
import functools

import numpy as np
import jax
import jax.numpy as jnp
from jax import lax
from jax.experimental import pallas as pl
from jax.experimental.pallas import tpu as pltpu

D_MODEL = 1024
GRID_W = 64
N_HEADS = 8
HEAD_DIM = 64
D_ATTN = N_HEADS * HEAD_DIM
D_CONV = 512
WIN_ROWS = 8
WIN_COLS = 16
D_FF = 2816
RMS_EPS = 1e-6
NEG_INF = -1e30
ATTN_SCALE = HEAD_DIM ** -0.5

LANES = 128
BF16_SUBLANES = 16
WIN_TOKENS = WIN_ROWS * GRID_W

TM_PROJ = 512
ROWS_ATTN = 16
TM_POST = 512
FF_CHUNK = 256

BF16 = jnp.bfloat16
F32 = jnp.float32


def _const_spec(shape):
    return pl.BlockSpec(shape, lambda *_: (0,) * len(shape), pipeline_mode=pl.Buffered(1))


def _rms_norm(x, g):
    return x * lax.rsqrt(jnp.mean(x * x, axis=-1, keepdims=True) + RMS_EPS) * g


def _inproj_kernel(x_ref, g_ref, w_ref, q_ref, k_ref, v_ref, z_ref, bg_ref, gl_ref):
    xn = _rms_norm(x_ref[0], g_ref[...]).astype(BF16)

    def proj(lo, hi):
        return jnp.dot(xn, w_ref[:, lo:hi], preferred_element_type=F32)

    q_ref[0] = (proj(0, D_ATTN) * ATTN_SCALE).astype(BF16)
    k_ref[0] = proj(D_ATTN, 2 * D_ATTN).astype(BF16)
    v_ref[0] = proj(2 * D_ATTN, 3 * D_ATTN).astype(BF16)
    o = 3 * D_ATTN
    u = proj(o, o + D_CONV)
    bg_ref[0] = proj(o + D_CONV, o + 2 * D_CONV).astype(BF16)
    cg = proj(o + 2 * D_CONV, o + 3 * D_CONV)
    z_ref[0] = (cg * u).astype(BF16)
    o += 3 * D_CONV
    for c in range(0, 2 * D_MODEL, 1024):
        gl_ref[0, :, c:c + 1024] = proj(o + c, o + c + 1024).astype(BF16)


def _inproj(x, g, w_in):
    b, l, _ = x.shape
    d_in = w_in.shape[1]
    tok = lambda width: pl.BlockSpec((1, TM_PROJ, width), lambda bi, i: (bi, i, 0))
    out_widths = (D_ATTN, D_ATTN, D_ATTN, D_CONV, D_CONV, 2 * D_MODEL)
    return pl.pallas_call(
        _inproj_kernel,
        grid=(b, l // TM_PROJ),
        in_specs=[tok(D_MODEL), _const_spec((1, D_MODEL)), _const_spec((D_MODEL, d_in))],
        out_specs=[tok(w) for w in out_widths],
        out_shape=[jax.ShapeDtypeStruct((b, l, w), BF16) for w in out_widths],
        compiler_params=pltpu.CompilerParams(
            dimension_semantics=("parallel", "parallel"), vmem_limit_bytes=48 * 2**20),
        name="inproj",
    )(x, g, w_in)


def _attn_kernel(q_ref, k_ref, v_ref, bias_ref, o_ref, s_scr, p_scr, l_scr, *, rows):
    row0 = pl.program_id(1) * ROWS_ATTN
    lane = lax.broadcasted_iota(jnp.int32, (GRID_W, LANES), 1)
    low_half = lane < HEAD_DIM
    n_pairs = N_HEADS // 2

    def row_body(i, carry):
        r = row0 + i
        rs = jnp.clip(r - WIN_ROWS // 2, 0, rows - WIN_ROWS)
        delta = r - rs
        q_off = pl.multiple_of(i * GRID_W, GRID_W)
        k_off = pl.multiple_of(rs * GRID_W, GRID_W)
        for hp in range(n_pairs):
            cols = slice(hp * LANES, (hp + 1) * LANES)
            q_pair = q_ref[0, pl.ds(q_off, GRID_W), cols]
            k_pair = k_ref[0, pl.ds(k_off, WIN_TOKENS), cols]
            zero = jnp.zeros_like(q_pair)
            q_both = jnp.concatenate([jnp.where(low_half, q_pair, zero),
                                      jnp.where(low_half, zero, q_pair)], axis=0)
            s = lax.dot_general(q_both, k_pair, (((1,), (1,)), ((), ())),
                                preferred_element_type=F32)
            s_scr[hp] = s + bias_ref[delta, hp]
        for hp in range(n_pairs):
            s = s_scr[hp]
            m = jnp.max(s, axis=-1, keepdims=True)
            p = jnp.exp(s - m)
            l_scr[hp] = jnp.broadcast_to(jnp.sum(p, axis=-1, keepdims=True), (2 * GRID_W, LANES))
            p_scr[hp] = p.astype(BF16)
        for hp in range(n_pairs):
            cols = slice(hp * LANES, (hp + 1) * LANES)
            v_pair = v_ref[0, pl.ds(k_off, WIN_TOKENS), cols]
            o = jnp.dot(p_scr[hp], v_pair, preferred_element_type=F32) / l_scr[hp]
            o_ref[0, pl.ds(q_off, GRID_W), cols] = jnp.where(low_half, o[:GRID_W], o[GRID_W:]).astype(BF16)
        return carry

    lax.fori_loop(0, ROWS_ATTN, row_body, 0)


def _attention(q, k, v, bias):
    b, l, _ = q.shape
    rows = l // GRID_W
    tm = ROWS_ATTN * GRID_W
    tile = pl.BlockSpec((1, tm, D_ATTN), lambda bi, i: (bi, i, 0))
    whole = pl.BlockSpec((1, l, D_ATTN), lambda bi, i: (bi, 0, 0), pipeline_mode=pl.Buffered(1))
    return pl.pallas_call(
        functools.partial(_attn_kernel, rows=rows),
        grid=(b, rows // ROWS_ATTN),
        in_specs=[tile, whole, whole, _const_spec(bias.shape)],
        out_specs=tile,
        out_shape=jax.ShapeDtypeStruct((b, l, D_ATTN), BF16),
        scratch_shapes=[pltpu.VMEM((N_HEADS // 2, 2 * GRID_W, WIN_TOKENS), F32),
                        pltpu.VMEM((N_HEADS // 2, 2 * GRID_W, WIN_TOKENS), BF16),
                        pltpu.VMEM((N_HEADS // 2, 2 * GRID_W, LANES), F32)],
        compiler_params=pltpu.CompilerParams(
            dimension_semantics=("parallel", "arbitrary"), vmem_limit_bytes=48 * 2**20),
        name="natten",
    )(q, k, v, bias)


def _bias_table(rpb):
    qcol = np.arange(GRID_W)[:, None]
    kcol = np.arange(GRID_W)[None, :]
    cstart = np.clip(qcol - WIN_COLS // 2, 0, GRID_W - WIN_COLS)
    valid = (kcol >= cstart) & (kcol < cstart + WIN_COLS)
    cidx = np.clip(kcol - qcol + WIN_COLS - 1, 0, 2 * WIN_COLS - 2)
    t = jnp.where(valid[None, None], rpb[:, :, cidx].astype(F32), NEG_INF)
    tabs = []
    for delta in range(WIN_ROWS):
        sl = t[:, WIN_ROWS - 1 - delta: 2 * WIN_ROWS - 1 - delta]
        tabs.append(sl.transpose(0, 2, 1, 3).reshape(N_HEADS // 2, 2 * GRID_W, WIN_TOKENS))
    return jnp.stack(tabs)


def _post_kernel(x_ref, a_ref, z_ref, zp_ref, zn_ref, bg_ref, gl_ref, bgate_ref, convw_ref,
                 wa_ref, wc_ref, wo_ref, g2_ref, wg_ref, wu_ref, wd_ref, g3_ref, y_ref):
    i = pl.program_id(1)
    n = pl.num_programs(1)
    tm = z_ref.shape[1]

    z = z_ref[0].astype(F32)
    prev_row = jnp.where(i > 0, zp_ref[0, BF16_SUBLANES - 1:BF16_SUBLANES, :].astype(F32), 0.0)
    next_row = jnp.where(i < n - 1, zn_ref[0, 0:1, :].astype(F32), 0.0)
    ridx = lax.broadcasted_iota(jnp.int32, z.shape, 0)
    z_m1 = jnp.where(ridx == 0, prev_row, pltpu.roll(z, 1, 0))
    z_p1 = jnp.where(ridx == tm - 1, next_row, pltpu.roll(z, tm - 1, 0))
    cw = convw_ref[...]
    c = bg_ref[0].astype(F32) * (z_m1 * cw[0:1] + z * cw[1:2] + z_p1 * cw[2:3])

    gl = gl_ref[0].astype(F32) + bgate_ref[...]
    gates = jax.nn.sigmoid(gl)
    pa = jnp.dot(a_ref[0], wa_ref[...], preferred_element_type=F32)
    pc = jnp.dot(c.astype(BF16), wc_ref[...], preferred_element_type=F32)
    merged = gates[:, :D_MODEL] * pa + gates[:, D_MODEL:] * pc
    x1 = x_ref[0] + jnp.dot(merged.astype(BF16), wo_ref[...], preferred_element_type=F32)

    xn = _rms_norm(x1, g2_ref[...]).astype(BF16)
    acc = x1
    for f in range(0, D_FF, FF_CHUNK):
        gate = jnp.dot(xn, wg_ref[:, f:f + FF_CHUNK], preferred_element_type=F32)
        up = jnp.dot(xn, wu_ref[:, f:f + FF_CHUNK], preferred_element_type=F32)
        act = (gate * jax.nn.sigmoid(gate) * up).astype(BF16)
        acc = acc + jnp.dot(act, wd_ref[f:f + FF_CHUNK, :], preferred_element_type=F32)
    y_ref[0] = _rms_norm(acc, g3_ref[...])


def _post(x, a, z, bg, gl, b_gate, conv_w, wa, wc, wo, g2, wg, wu, wd, g3):
    b, l, _ = x.shape
    halo = TM_POST // BF16_SUBLANES
    n_halo = l // BF16_SUBLANES
    tok = lambda width: pl.BlockSpec((1, TM_POST, width), lambda bi, i: (bi, i, 0))
    z_prev = pl.BlockSpec((1, BF16_SUBLANES, D_CONV),
                          lambda bi, i: (bi, jnp.maximum(i * halo - 1, 0), 0))
    z_next = pl.BlockSpec((1, BF16_SUBLANES, D_CONV),
                          lambda bi, i: (bi, jnp.minimum((i + 1) * halo, n_halo - 1), 0))
    consts = (b_gate, conv_w, wa, wc, wo, g2, wg, wu, wd, g3)
    return pl.pallas_call(
        _post_kernel,
        grid=(b, l // TM_POST),
        in_specs=[tok(D_MODEL), tok(D_ATTN), tok(D_CONV), z_prev, z_next, tok(D_CONV),
                  tok(2 * D_MODEL)] + [_const_spec(c.shape) for c in consts],
        out_specs=tok(D_MODEL),
        out_shape=jax.ShapeDtypeStruct((b, l, D_MODEL), F32),
        compiler_params=pltpu.CompilerParams(
            dimension_semantics=("parallel", "parallel"), vmem_limit_bytes=56 * 2**20),
        name="post",
    )(x, a, z, z, z, bg, gl, *consts)


def kernel(x_prompt, x_sample, norm_mix_g, w_in, b_gate, rpb, conv_w, w_attn_branch, w_conv_branch,
           w_out, norm_ffn_g, w_ffn_in, w_ffn_down, norm_final_g):
    assert w_in.shape[0] == 1, "the final norm is fused into the single layer's last kernel"
    bias = _bias_table(rpb[0])
    w_in_b = w_in[0].astype(BF16)
    wa_b = w_attn_branch[0].astype(BF16)
    wc_b = w_conv_branch[0].astype(BF16)
    wo_b = w_out[0].astype(BF16)
    wg_b = w_ffn_in[0, :, :D_FF].astype(BF16)
    wu_b = w_ffn_in[0, :, D_FF:].astype(BF16)
    wd_b = w_ffn_down[0].astype(BF16)
    g1 = norm_mix_g[0].reshape(1, D_MODEL)
    g2 = norm_ffn_g[0].reshape(1, D_MODEL)
    g3 = norm_final_g.reshape(1, D_MODEL)
    bgate = b_gate[0].reshape(1, 2 * D_MODEL)

    def trunk(x):
        q, k, v, z, bg, gl = _inproj(x, g1, w_in_b)
        a = _attention(q, k, v, bias)
        return _post(x, a, z, bg, gl, bgate, conv_w[0], wa_b, wc_b, wo_b, g2, wg_b, wu_b, wd_b, g3)

    return trunk(x_prompt), trunk(x_sample)
```

```python
import functools

import numpy as np
import jax
import jax.numpy as jnp
from jax import lax
from jax.experimental import pallas as pl
from jax.experimental.pallas import tpu as pltpu

D_MODEL = 1024
GRID_W = 64
N_HEADS = 8
HEAD_DIM = 64
D_ATTN = N_HEADS * HEAD_DIM
D_CONV = 512
WIN_ROWS = 8
WIN_COLS = 16
D_FF = 2816
RMS_EPS = 1e-6
NEG_INF = -1e30
LOG2E = 1.4426950408889634
Q_SCALE = HEAD_DIM ** -0.5 * LOG2E

LANES = 128
BF16_SUBLANES = 16
WIN_TOKENS = WIN_ROWS * GRID_W

TM_PROJ = 512
ROWS_ATTN = 16
TM_POST = 512
FF_CHUNK = 256

BF16 = jnp.bfloat16
F32 = jnp.float32


def _const_spec(shape):
    return pl.BlockSpec(shape, lambda *_: (0,) * len(shape), pipeline_mode=pl.Buffered(1))


def _rms_norm(x, g):
    return x * lax.rsqrt(jnp.mean(x * x, axis=-1, keepdims=True) + RMS_EPS) * g


def _inproj_kernel(x_ref, g_ref, w_ref, q_ref, k_ref, v_ref, z_ref, bg_ref, gl_ref):
    xn = _rms_norm(x_ref[0], g_ref[...]).astype(BF16)

    def proj(lo, hi):
        return jnp.dot(xn, w_ref[:, lo:hi], preferred_element_type=F32)

    q_ref[0] = (proj(0, D_ATTN) * Q_SCALE).astype(BF16)
    k_ref[0] = proj(D_ATTN, 2 * D_ATTN).astype(BF16)
    v_ref[0] = proj(2 * D_ATTN, 3 * D_ATTN).astype(BF16)
    o = 3 * D_ATTN
    u = proj(o, o + D_CONV)
    bg_ref[0] = proj(o + D_CONV, o + 2 * D_CONV).astype(BF16)
    cg = proj(o + 2 * D_CONV, o + 3 * D_CONV)
    z_ref[0] = (cg * u).astype(BF16)
    o += 3 * D_CONV
    for c in range(0, 2 * D_MODEL, 1024):
        gl_ref[0, :, c:c + 1024] = proj(o + c, o + c + 1024).astype(BF16)


def _inproj(x, g, w_in):
    b, l, _ = x.shape
    d_in = w_in.shape[1]
    tok = lambda width: pl.BlockSpec((1, TM_PROJ, width), lambda bi, i: (bi, i, 0))
    out_widths = (D_ATTN, D_ATTN, D_ATTN, D_CONV, D_CONV, 2 * D_MODEL)
    return pl.pallas_call(
        _inproj_kernel,
        grid=(b, l // TM_PROJ),
        in_specs=[tok(D_MODEL), _const_spec((1, D_MODEL)), _const_spec((D_MODEL, d_in))],
        out_specs=[tok(w) for w in out_widths],
        out_shape=[jax.ShapeDtypeStruct((b, l, w), BF16) for w in out_widths],
        compiler_params=pltpu.CompilerParams(
            dimension_semantics=("parallel", "parallel"), vmem_limit_bytes=48 * 2**20),
        name="inproj",
    )(x, g, w_in)


def _attn_kernel(q_ref, k_ref, v_ref, bias_ref, o_ref, s_scr, p_scr, l_scr, *, rows):
    row0 = pl.program_id(1) * ROWS_ATTN
    lane = lax.broadcasted_iota(jnp.int32, (GRID_W, LANES), 1)
    low_half = lane < HEAD_DIM
    n_pairs = N_HEADS // 2

    def row_body(i, carry):
        r = row0 + i
        rs = jnp.clip(r - WIN_ROWS // 2, 0, rows - WIN_ROWS)
        delta = r - rs
        q_off = pl.multiple_of(i * GRID_W, GRID_W)
        k_off = pl.multiple_of(rs * GRID_W, GRID_W)
        for hp in range(n_pairs):
            cols = slice(hp * LANES, (hp + 1) * LANES)
            q_pair = q_ref[0, pl.ds(q_off, GRID_W), cols]
            k_pair = k_ref[0, pl.ds(k_off, WIN_TOKENS), cols]
            zero = jnp.zeros_like(q_pair)
            q_both = jnp.concatenate([jnp.where(low_half, q_pair, zero),
                                      jnp.where(low_half, zero, q_pair)], axis=0)
            s = lax.dot_general(q_both, k_pair, (((1,), (1,)), ((), ())),
                                preferred_element_type=F32)
            s_scr[hp] = s + bias_ref[delta, hp]
        for hp in range(n_pairs):
            s = s_scr[hp]
            m = jnp.max(s, axis=-1, keepdims=True)
            p = jnp.exp2(s - m)
            l_scr[hp] = jnp.broadcast_to(jnp.sum(p, axis=-1, keepdims=True), (2 * GRID_W, LANES))
            p_scr[hp] = p.astype(BF16)
        for hp in range(n_pairs):
            cols = slice(hp * LANES, (hp + 1) * LANES)
            v_pair = v_ref[0, pl.ds(k_off, WIN_TOKENS), cols]
            o = jnp.dot(p_scr[hp], v_pair, preferred_element_type=F32) / l_scr[hp]
            o_ref[0, pl.ds(q_off, GRID_W), cols] = jnp.where(low_half, o[:GRID_W], o[GRID_W:]).astype(BF16)
        return carry

    lax.fori_loop(0, ROWS_ATTN, row_body, 0, unroll=8)


def _attention(q, k, v, bias):
    b, l, _ = q.shape
    rows = l // GRID_W
    tm = ROWS_ATTN * GRID_W
    tile = pl.BlockSpec((1, tm, D_ATTN), lambda bi, i: (bi, i, 0))
    whole = pl.BlockSpec((1, l, D_ATTN), lambda bi, i: (bi, 0, 0), pipeline_mode=pl.Buffered(1))
    return pl.pallas_call(
        functools.partial(_attn_kernel, rows=rows),
        grid=(b, rows // ROWS_ATTN),
        in_specs=[tile, whole, whole, _const_spec(bias.shape)],
        out_specs=tile,
        out_shape=jax.ShapeDtypeStruct((b, l, D_ATTN), BF16),
        scratch_shapes=[pltpu.VMEM((N_HEADS // 2, 2 * GRID_W, WIN_TOKENS), F32),
                        pltpu.VMEM((N_HEADS // 2, 2 * GRID_W, WIN_TOKENS), BF16),
                        pltpu.VMEM((N_HEADS // 2, 2 * GRID_W, LANES), F32)],
        compiler_params=pltpu.CompilerParams(
            dimension_semantics=("parallel", "arbitrary"), vmem_limit_bytes=48 * 2**20),
        name="natten",
    )(q, k, v, bias)


def _bias_table(rpb):
    qcol = np.arange(GRID_W)[:, None]
    kcol = np.arange(GRID_W)[None, :]
    cstart = np.clip(qcol - WIN_COLS // 2, 0, GRID_W - WIN_COLS)
    valid = (kcol >= cstart) & (kcol < cstart + WIN_COLS)
    cidx = np.clip(kcol - qcol + WIN_COLS - 1, 0, 2 * WIN_COLS - 2)
    t = jnp.where(valid[None, None], rpb[:, :, cidx].astype(F32) * LOG2E, NEG_INF)
    tabs = []
    for delta in range(WIN_ROWS):
        sl = t[:, WIN_ROWS - 1 - delta: 2 * WIN_ROWS - 1 - delta]
        tabs.append(sl.transpose(0, 2, 1, 3).reshape(N_HEADS // 2, 2 * GRID_W, WIN_TOKENS))
    return jnp.stack(tabs)


def _post_kernel(x_ref, a_ref, z_ref, zp_ref, zn_ref, bg_ref, gl_ref, bgate_ref, convw_ref,
                 wa_ref, wc_ref, wo_ref, g2_ref, wg_ref, wu_ref, wd_ref, g3_ref, y_ref):
    i = pl.program_id(1)
    n = pl.num_programs(1)
    tm = z_ref.shape[1]

    z = z_ref[0].astype(F32)
    prev_row = jnp.where(i > 0, zp_ref[0, BF16_SUBLANES - 1:BF16_SUBLANES, :].astype(F32), 0.0)
    next_row = jnp.where(i < n - 1, zn_ref[0, 0:1, :].astype(F32), 0.0)
    ridx = lax.broadcasted_iota(jnp.int32, z.shape, 0)
    z_m1 = jnp.where(ridx == 0, prev_row, pltpu.roll(z, 1, 0))
    z_p1 = jnp.where(ridx == tm - 1, next_row, pltpu.roll(z, tm - 1, 0))
    cw = convw_ref[...]
    c = bg_ref[0].astype(F32) * (z_m1 * cw[0:1] + z * cw[1:2] + z_p1 * cw[2:3])

    gl = gl_ref[0].astype(F32) + bgate_ref[...]
    gates = jax.nn.sigmoid(gl)
    pa = jnp.dot(a_ref[0], wa_ref[...], preferred_element_type=F32)
    pc = jnp.dot(c.astype(BF16), wc_ref[...], preferred_element_type=F32)
    merged = gates[:, :D_MODEL] * pa + gates[:, D_MODEL:] * pc
    x1 = x_ref[0] + jnp.dot(merged.astype(BF16), wo_ref[...], preferred_element_type=F32)

    xn = _rms_norm(x1, g2_ref[...]).astype(BF16)
    acc = x1
    for f in range(0, D_FF, FF_CHUNK):
        gate = jnp.dot(xn, wg_ref[:, f:f + FF_CHUNK], preferred_element_type=F32)
        up = jnp.dot(xn, wu_ref[:, f:f + FF_CHUNK], preferred_element_type=F32)
        act = (gate * jax.nn.sigmoid(gate) * up).astype(BF16)
        acc = acc + jnp.dot(act, wd_ref[f:f + FF_CHUNK, :], preferred_element_type=F32)
    y_ref[0] = _rms_norm(acc, g3_ref[...])


def _post(x, a, z, bg, gl, b_gate, conv_w, wa, wc, wo, g2, wg, wu, wd, g3):
    b, l, _ = x.shape
    halo = TM_POST // BF16_SUBLANES
    n_halo = l // BF16_SUBLANES
    tok = lambda width: pl.BlockSpec((1, TM_POST, width), lambda bi, i: (bi, i, 0))
    z_prev = pl.BlockSpec((1, BF16_SUBLANES, D_CONV),
                          lambda bi, i: (bi, jnp.maximum(i * halo - 1, 0), 0))
    z_next = pl.BlockSpec((1, BF16_SUBLANES, D_CONV),
                          lambda bi, i: (bi, jnp.minimum((i + 1) * halo, n_halo - 1), 0))
    consts = (b_gate, conv_w, wa, wc, wo, g2, wg, wu, wd, g3)
    return pl.pallas_call(
        _post_kernel,
        grid=(b, l // TM_POST),
        in_specs=[tok(D_MODEL), tok(D_ATTN), tok(D_CONV), z_prev, z_next, tok(D_CONV),
                  tok(2 * D_MODEL)] + [_const_spec(c.shape) for c in consts],
        out_specs=tok(D_MODEL),
        out_shape=jax.ShapeDtypeStruct((b, l, D_MODEL), F32),
        compiler_params=pltpu.CompilerParams(
            dimension_semantics=("parallel", "parallel"), vmem_limit_bytes=56 * 2**20),
        name="post",
    )(x, a, z, z, z, bg, gl, *consts)


def kernel(x_prompt, x_sample, norm_mix_g, w_in, b_gate, rpb, conv_w, w_attn_branch, w_conv_branch,
           w_out, norm_ffn_g, w_ffn_in, w_ffn_down, norm_final_g):
    assert w_in.shape[0] == 1, "the final norm is fused into the single layer's last kernel"
    bias = _bias_table(rpb[0])
    w_in_b = w_in[0].astype(BF16)
    wa_b = w_attn_branch[0].astype(BF16)
    wc_b = w_conv_branch[0].astype(BF16)
    wo_b = w_out[0].astype(BF16)
    wg_b = w_ffn_in[0, :, :D_FF].astype(BF16)
    wu_b = w_ffn_in[0, :, D_FF:].astype(BF16)
    wd_b = w_ffn_down[0].astype(BF16)
    g1 = norm_mix_g[0].reshape(1, D_MODEL)
    g2 = norm_ffn_g[0].reshape(1, D_MODEL)
    g3 = norm_final_g.reshape(1, D_MODEL)
    bgate = b_gate[0].reshape(1, 2 * D_MODEL)

    def trunk(x):
        q, k, v, z, bg, gl = _inproj(x, g1, w_in_b)
        a = _attention(q, k, v, bias)
        return _post(x, a, z, bg, gl, bgate, conv_w[0], wa_b, wc_b, wo_b, g2, wg_b, wu_b, wd_b, g3)

    return trunk(x_prompt), trunk(x_sample)
```

```python
import functools

import numpy as np
import jax
import jax.numpy as jnp
from jax import lax
from jax.experimental import pallas as pl
from jax.experimental.pallas import tpu as pltpu

D_MODEL = 1024
GRID_W = 64
N_HEADS = 8
HEAD_DIM = 64
D_ATTN = N_HEADS * HEAD_DIM
D_CONV = 512
WIN_ROWS = 8
WIN_COLS = 16
D_FF = 2816
RMS_EPS = 1e-6
NEG_INF = -1e30
LOG2E = 1.4426950408889634
Q_SCALE = HEAD_DIM ** -0.5 * LOG2E

LANES = 128
BF16_SUBLANES = 16
WIN_TOKENS = WIN_ROWS * GRID_W
N_PAIRS = N_HEADS // 2
BIAS_CHUNKS = (2 * WIN_ROWS - 1) * GRID_W // LANES

TM_PROJ = 512
TM_MIX = 512
ROWS_MIX = TM_MIX // GRID_W
HALO = (WIN_ROWS // 2) * GRID_W
FF_CHUNK = 256

BF16 = jnp.bfloat16
F32 = jnp.float32


def _const_spec(shape):
    return pl.BlockSpec(shape, lambda *_: (0,) * len(shape), pipeline_mode=pl.Buffered(1))


def _rms_norm(x, g):
    return x * lax.rsqrt(jnp.mean(x * x, axis=-1, keepdims=True) + RMS_EPS) * g


def _inproj_kernel(x_ref, g_ref, w_ref, q_ref, k_ref, v_ref, z_ref, bg_ref, gl_ref):
    xn = _rms_norm(x_ref[0], g_ref[...]).astype(BF16)

    def proj(lo, hi):
        return jnp.dot(xn, w_ref[:, lo:hi], preferred_element_type=F32)

    q_ref[0] = (proj(0, D_ATTN) * Q_SCALE).astype(BF16)
    k_ref[0] = proj(D_ATTN, 2 * D_ATTN).astype(BF16)
    v_ref[0] = proj(2 * D_ATTN, 3 * D_ATTN).astype(BF16)
    o = 3 * D_ATTN
    u = proj(o, o + D_CONV)
    bg_ref[0] = proj(o + D_CONV, o + 2 * D_CONV).astype(BF16)
    cg = proj(o + 2 * D_CONV, o + 3 * D_CONV)
    z_ref[0] = (cg * u).astype(BF16)
    o += 3 * D_CONV
    for c in range(0, 2 * D_MODEL, 1024):
        gl_ref[0, :, c:c + 1024] = proj(o + c, o + c + 1024).astype(BF16)


def _inproj(x, g, w_in):
    b, l, _ = x.shape
    d_in = w_in.shape[1]
    tok = lambda width: pl.BlockSpec((1, TM_PROJ, width), lambda bi, i: (bi, i, 0))
    out_widths = (D_ATTN, D_ATTN, D_ATTN, D_CONV, D_CONV, 2 * D_MODEL)
    return pl.pallas_call(
        _inproj_kernel,
        grid=(b, l // TM_PROJ),
        in_specs=[tok(D_MODEL), _const_spec((1, D_MODEL)), _const_spec((D_MODEL, d_in))],
        out_specs=[tok(w) for w in out_widths],
        out_shape=[jax.ShapeDtypeStruct((b, l, w), BF16) for w in out_widths],
        compiler_params=pltpu.CompilerParams(
            dimension_semantics=("parallel", "parallel"), vmem_limit_bytes=48 * 2**20),
        name="inproj",
    )(x, g, w_in)


def _attention_row(i, tile_in_seq, rows, q_ref, bias_ref, k_scr, v_scr, s_scr, p_scr, l_scr, a_ref):
    lane = lax.broadcasted_iota(jnp.int32, (GRID_W, LANES), 1)
    low_half = lane < HEAD_DIM
    row0 = tile_in_seq * ROWS_MIX
    r = row0 + i
    rs = jnp.clip(r - WIN_ROWS // 2, 0, rows - WIN_ROWS)
    shift = WIN_ROWS - 1 - (r - rs)
    parity = shift & 1
    chunk0 = shift >> 1
    k_off = pl.multiple_of((rs - row0 + WIN_ROWS // 2) * GRID_W, GRID_W)
    q_rows = slice(i * GRID_W, (i + 1) * GRID_W)
    for hp in range(N_PAIRS):
        cols = slice(hp * LANES, (hp + 1) * LANES)
        q_pair = q_ref[0, q_rows, cols]
        k_pair = k_scr[pl.ds(k_off, WIN_TOKENS), cols]
        zero = jnp.zeros_like(q_pair)
        q_both = jnp.concatenate([jnp.where(low_half, q_pair, zero),
                                  jnp.where(low_half, zero, q_pair)], axis=0)
        s = lax.dot_general(q_both, k_pair, (((1,), (1,)), ((), ())),
                            preferred_element_type=F32)
        bias = jnp.concatenate([bias_ref[parity, hp, chunk0 + j] for j in range(WIN_TOKENS // LANES)],
                               axis=1)
        s_scr[hp] = s + bias
    for hp in range(N_PAIRS):
        s = s_scr[hp]
        m = jnp.max(s, axis=-1, keepdims=True)
        p = jnp.exp2(s - m)
        l_scr[hp] = jnp.broadcast_to(jnp.sum(p, axis=-1, keepdims=True), (2 * GRID_W, LANES))
        p_scr[hp] = p.astype(BF16)
    for hp in range(N_PAIRS):
        cols = slice(hp * LANES, (hp + 1) * LANES)
        v_pair = v_scr[pl.ds(k_off, WIN_TOKENS), cols]
        o = jnp.dot(p_scr[hp], v_pair, preferred_element_type=F32) / l_scr[hp]
        a_ref[q_rows, cols] = jnp.where(low_half, o[:GRID_W], o[GRID_W:]).astype(BF16)


def _post_tile(tile_in_seq, tiles_per_seq, a_ref, x_ref, z_ref, zp_ref, zn_ref, bg_ref, gl_ref, bgate_ref,
               convw_ref, wa_ref, wc_ref, wo_ref, g2_ref, wgu_ref, wd_ref, g3_ref, y_ref, attention_row):
    z = z_ref[0].astype(F32)
    prev_row = jnp.where(tile_in_seq > 0, zp_ref[0, BF16_SUBLANES - 1:BF16_SUBLANES, :].astype(F32), 0.0)
    next_row = jnp.where(tile_in_seq < tiles_per_seq - 1, zn_ref[0, 0:1, :].astype(F32), 0.0)
    ridx = lax.broadcasted_iota(jnp.int32, z.shape, 0)
    z_m1 = jnp.where(ridx == 0, prev_row, pltpu.roll(z, 1, 0))
    z_p1 = jnp.where(ridx == TM_MIX - 1, next_row, pltpu.roll(z, TM_MIX - 1, 0))
    cw = convw_ref[...]
    c = bg_ref[0].astype(F32) * (z_m1 * cw[0:1] + z * cw[1:2] + z_p1 * cw[2:3])

    gl = gl_ref[0].astype(F32) + bgate_ref[...]
    gates = jax.nn.sigmoid(gl)
    pa = jnp.dot(a_ref[...], wa_ref[...], preferred_element_type=F32)
    pc = jnp.dot(c.astype(BF16), wc_ref[...], preferred_element_type=F32)
    merged = gates[:, :D_MODEL] * pa + gates[:, D_MODEL:] * pc
    x1 = x_ref[0] + jnp.dot(merged.astype(BF16), wo_ref[...], preferred_element_type=F32)

    xn = _rms_norm(x1, g2_ref[...]).astype(BF16)
    acc = x1
    for ci, f in enumerate(range(0, D_FF, FF_CHUNK)):
        gate = jnp.dot(xn, wgu_ref[:, f:f + FF_CHUNK], preferred_element_type=F32)
        up = jnp.dot(xn, wgu_ref[:, D_FF + f:D_FF + f + FF_CHUNK], preferred_element_type=F32)
        act = (gate * jax.nn.sigmoid(gate) * up).astype(BF16)
        acc = acc + jnp.dot(act, wd_ref[f:f + FF_CHUNK, :], preferred_element_type=F32)
        if ci < ROWS_MIX:
            attention_row(ci)
    y_ref[0] = _rms_norm(acc, g3_ref[...])


def _mix_kernel(q_ref, kp_ref, km_ref, kn_ref, vp_ref, vm_ref, vn_ref, bias_ref,
                x_ref, z_ref, zp_ref, zn_ref, bg_ref, gl_ref, bgate_ref, convw_ref,
                wa_ref, wc_ref, wo_ref, g2_ref, wgu_ref, wd_ref, g3_ref,
                y_ref,
                k_scr, v_scr, s_scr, p_scr, l_scr, a_next, a_cur, *, tiles_per_seq, n_tiles):
    t = pl.program_id(0)

    @pl.when(t == 0)
    def _():
        a_next[...] = jnp.zeros_like(a_next)

    a_cur[...] = a_next[...]

    for scr, prev, main, nxt in ((k_scr, kp_ref, km_ref, kn_ref), (v_scr, vp_ref, vm_ref, vn_ref)):
        scr[0:HALO] = prev[0]
        scr[HALO:HALO + TM_MIX] = main[0]
        scr[HALO + TM_MIX:] = nxt[0]
    attn_tile = jnp.minimum(t, n_tiles - 1) % tiles_per_seq
    attention_row = functools.partial(
        _attention_row, tile_in_seq=attn_tile, rows=tiles_per_seq * ROWS_MIX, q_ref=q_ref, bias_ref=bias_ref,
        k_scr=k_scr, v_scr=v_scr, s_scr=s_scr, p_scr=p_scr, l_scr=l_scr, a_ref=a_next)

    post_tile = jnp.maximum(t - 1, 0) % tiles_per_seq
    _post_tile(post_tile, tiles_per_seq, a_cur, x_ref, z_ref, zp_ref, zn_ref, bg_ref, gl_ref, bgate_ref,
               convw_ref, wa_ref, wc_ref, wo_ref, g2_ref, wgu_ref, wd_ref, g3_ref, y_ref, attention_row)


def _mix(x, q, k, v, z, bg, gl, bias, consts):
    b, l, _ = x.shape
    tiles_per_seq = l // TM_MIX
    n_tiles = b * tiles_per_seq
    halo_per_tile = TM_MIX // HALO
    n_halo = l // HALO
    zrow_per_tile = TM_MIX // BF16_SUBLANES
    n_zrow = l // BF16_SUBLANES

    def attn_tile(t):
        ta = jnp.minimum(t, n_tiles - 1)
        return ta // tiles_per_seq, ta % tiles_per_seq

    def post_tile(t):
        tp = jnp.maximum(t - 1, 0)
        return tp // tiles_per_seq, tp % tiles_per_seq

    def a_spec(width):
        return pl.BlockSpec((1, TM_MIX, width), lambda t: (*attn_tile(t), 0))

    def p_spec(width):
        return pl.BlockSpec((1, TM_MIX, width), lambda t: (*post_tile(t), 0))

    def halo_prev(t):
        bi, g = attn_tile(t)
        return bi, jnp.maximum(g * halo_per_tile - 1, 0), 0

    def halo_next(t):
        bi, g = attn_tile(t)
        return bi, jnp.minimum((g + 1) * halo_per_tile, n_halo - 1), 0

    def z_prev(t):
        bi, g = post_tile(t)
        return bi, jnp.maximum(g * zrow_per_tile - 1, 0), 0

    def z_next(t):
        bi, g = post_tile(t)
        return bi, jnp.minimum((g + 1) * zrow_per_tile, n_zrow - 1), 0

    kv_halo = lambda index_map: pl.BlockSpec((1, HALO, D_ATTN), index_map)
    z_halo = lambda index_map: pl.BlockSpec((1, BF16_SUBLANES, D_CONV), index_map)
    kv_specs = [kv_halo(halo_prev), a_spec(D_ATTN), kv_halo(halo_next)]
    return pl.pallas_call(
        functools.partial(_mix_kernel, tiles_per_seq=tiles_per_seq, n_tiles=n_tiles),
        grid=(n_tiles + 1,),
        in_specs=[a_spec(D_ATTN)] + kv_specs + kv_specs + [_const_spec(bias.shape)]
                 + [p_spec(D_MODEL), p_spec(D_CONV), z_halo(z_prev), z_halo(z_next), p_spec(D_CONV),
                    p_spec(2 * D_MODEL)] + [_const_spec(c.shape) for c in consts],
        out_specs=p_spec(D_MODEL),
        out_shape=jax.ShapeDtypeStruct((b, l, D_MODEL), F32),
        scratch_shapes=[pltpu.VMEM((TM_MIX + 2 * HALO, D_ATTN), BF16),
                        pltpu.VMEM((TM_MIX + 2 * HALO, D_ATTN), BF16),
                        pltpu.VMEM((N_PAIRS, 2 * GRID_W, WIN_TOKENS), F32),
                        pltpu.VMEM((N_PAIRS, 2 * GRID_W, WIN_TOKENS), BF16),
                        pltpu.VMEM((N_PAIRS, 2 * GRID_W, LANES), F32),
                        pltpu.VMEM((TM_MIX, D_ATTN), BF16),
                        pltpu.VMEM((TM_MIX, D_ATTN), BF16)],
        compiler_params=pltpu.CompilerParams(
            dimension_semantics=("arbitrary",), vmem_limit_bytes=60 * 2**20),
        name="mix",
    )(q, k, k, k, v, v, v, bias, x, z, z, z, bg, gl, *consts)


def _bias_table(rpb):
    qcol = np.arange(GRID_W)[:, None]
    kcol = np.arange(GRID_W)[None, :]
    cstart = np.clip(qcol - WIN_COLS // 2, 0, GRID_W - WIN_COLS)
    valid = (kcol >= cstart) & (kcol < cstart + WIN_COLS)
    cidx = np.clip(kcol - qcol + WIN_COLS - 1, 0, 2 * WIN_COLS - 2)
    t = jnp.where(valid[None, None], rpb[:, :, cidx].astype(F32) * LOG2E, NEG_INF)
    n_ri = 2 * WIN_ROWS - 1
    full = t.transpose(0, 2, 1, 3).reshape(N_PAIRS, 2 * GRID_W, n_ri * GRID_W)
    width = BIAS_CHUNKS * LANES
    tabs = [full[:, :, par * GRID_W: par * GRID_W + width]
            .reshape(N_PAIRS, 2 * GRID_W, BIAS_CHUNKS, LANES).transpose(0, 2, 1, 3) for par in range(2)]
    return jnp.stack(tabs)


def kernel(x_prompt, x_sample, norm_mix_g, w_in, b_gate, rpb, conv_w, w_attn_branch, w_conv_branch,
           w_out, norm_ffn_g, w_ffn_in, w_ffn_down, norm_final_g):
    assert w_in.shape[0] == 1, "the final norm is fused into the single layer's last kernel"
    bias = _bias_table(rpb[0])
    w_in_b = w_in[0].astype(BF16)
    g1 = norm_mix_g[0].reshape(1, D_MODEL)
    consts = (b_gate[0].reshape(1, 2 * D_MODEL), conv_w[0], w_attn_branch[0].astype(BF16),
              w_conv_branch[0].astype(BF16), w_out[0].astype(BF16), norm_ffn_g[0].reshape(1, D_MODEL),
              w_ffn_in[0].astype(BF16), w_ffn_down[0].astype(BF16), norm_final_g.reshape(1, D_MODEL))

    def trunk(x):
        q, k, v, z, bg, gl = _inproj(x, g1, w_in_b)
        return _mix(x, q, k, v, z, bg, gl, bias, consts)

    return trunk(x_prompt), trunk(x_sample)
```

```python
import functools

import numpy as np
import jax
import jax.numpy as jnp
from jax import lax
from jax.experimental import pallas as pl
from jax.experimental.pallas import tpu as pltpu

D_MODEL = 1024
GRID_W = 64
N_HEADS = 8
HEAD_DIM = 64
D_ATTN = N_HEADS * HEAD_DIM
D_CONV = 512
WIN_ROWS = 8
WIN_COLS = 16
D_FF = 2816
RMS_EPS = 1e-6
NEG_INF = -1e30
LOG2E = 1.4426950408889634
Q_SCALE = HEAD_DIM ** -0.5 * LOG2E

LANES = 128
BF16_SUBLANES = 16
WIN_TOKENS = WIN_ROWS * GRID_W
N_PAIRS = N_HEADS // 2
BIAS_CHUNKS = (2 * WIN_ROWS - 1) * GRID_W // LANES

TM_PROJ = 512
TM_MIX = 512
ROWS_MIX = TM_MIX // GRID_W
HALO = (WIN_ROWS // 2) * GRID_W
FF_CHUNK = 256

BF16 = jnp.bfloat16
F32 = jnp.float32


def _const_spec(shape):
    return pl.BlockSpec(shape, lambda *_: (0,) * len(shape), pipeline_mode=pl.Buffered(1))


def _sigmoid(x):
    return 0.5 * jnp.tanh(0.5 * x) + 0.5


def _rstd(x):
    return lax.rsqrt(jnp.mean(x * x, axis=-1, keepdims=True) + RMS_EPS)


def _inproj_kernel(x_ref, g_ref, w_ref, q_ref, k_ref, v_ref, z_ref, bg_ref, gl_ref):
    x = x_ref[0]
    xg = (x * g_ref[...]).astype(BF16)
    rstd = _rstd(x)

    def proj(lo, hi):
        return jnp.dot(xg, w_ref[:, lo:hi], preferred_element_type=F32) * rstd

    q_ref[0] = (proj(0, D_ATTN) * Q_SCALE).astype(BF16)
    k_ref[0] = proj(D_ATTN, 2 * D_ATTN).astype(BF16)
    v_ref[0] = proj(2 * D_ATTN, 3 * D_ATTN).astype(BF16)
    o = 3 * D_ATTN
    u = proj(o, o + D_CONV)
    bg_ref[0] = proj(o + D_CONV, o + 2 * D_CONV).astype(BF16)
    cg = proj(o + 2 * D_CONV, o + 3 * D_CONV)
    z_ref[0] = (cg * u).astype(BF16)
    o += 3 * D_CONV
    for c in range(0, 2 * D_MODEL, 1024):
        gl_ref[0, :, c:c + 1024] = proj(o + c, o + c + 1024).astype(BF16)


def _inproj(x, g, w_in):
    b, l, _ = x.shape
    d_in = w_in.shape[1]
    tok = lambda width: pl.BlockSpec((1, TM_PROJ, width), lambda bi, i: (bi, i, 0))
    out_widths = (D_ATTN, D_ATTN, D_ATTN, D_CONV, D_CONV, 2 * D_MODEL)
    return pl.pallas_call(
        _inproj_kernel,
        grid=(b, l // TM_PROJ),
        in_specs=[tok(D_MODEL), _const_spec((1, D_MODEL)), _const_spec((D_MODEL, d_in))],
        out_specs=[tok(w) for w in out_widths],
        out_shape=[jax.ShapeDtypeStruct((b, l, w), BF16) for w in out_widths],
        compiler_params=pltpu.CompilerParams(
            dimension_semantics=("parallel", "parallel"), vmem_limit_bytes=48 * 2**20),
        name="inproj",
    )(x, g, w_in)


def _attention_row(i, tile_in_seq, rows, q_ref, bias_ref, k_scr, v_scr, s_scr, p_scr, l_scr, a_ref):
    lane = lax.broadcasted_iota(jnp.int32, (GRID_W, LANES), 1)
    low_half = lane < HEAD_DIM
    row0 = tile_in_seq * ROWS_MIX
    r = row0 + i
    rs = jnp.clip(r - WIN_ROWS // 2, 0, rows - WIN_ROWS)
    shift = WIN_ROWS - 1 - (r - rs)
    parity = shift & 1
    chunk0 = shift >> 1
    k_off = pl.multiple_of((rs - row0 + WIN_ROWS // 2) * GRID_W, GRID_W)
    q_rows = slice(i * GRID_W, (i + 1) * GRID_W)
    for hp in range(N_PAIRS):
        cols = slice(hp * LANES, (hp + 1) * LANES)
        q_pair = q_ref[0, q_rows, cols]
        k_pair = k_scr[pl.ds(k_off, WIN_TOKENS), cols]
        zero = jnp.zeros_like(q_pair)
        q_both = jnp.concatenate([jnp.where(low_half, q_pair, zero),
                                  jnp.where(low_half, zero, q_pair)], axis=0)
        s = lax.dot_general(q_both, k_pair, (((1,), (1,)), ((), ())),
                            preferred_element_type=F32)
        bias = jnp.concatenate([bias_ref[parity, hp, chunk0 + j] for j in range(WIN_TOKENS // LANES)],
                               axis=1)
        s_scr[hp] = s + bias
    for hp in range(N_PAIRS):
        s = s_scr[hp]
        m = jnp.max(s, axis=-1, keepdims=True)
        p = jnp.exp2(s - m)
        l_scr[hp] = jnp.broadcast_to(jnp.sum(p, axis=-1, keepdims=True), (2 * GRID_W, LANES))
        p_scr[hp] = p.astype(BF16)
    for hp in range(N_PAIRS):
        cols = slice(hp * LANES, (hp + 1) * LANES)
        v_pair = v_scr[pl.ds(k_off, WIN_TOKENS), cols]
        o = jnp.dot(p_scr[hp], v_pair, preferred_element_type=F32) / l_scr[hp]
        a_ref[q_rows, cols] = jnp.where(low_half, o[:GRID_W], o[GRID_W:]).astype(BF16)


def _post_tile(tile_in_seq, tiles_per_seq, a_ref, x_ref, z_ref, zp_ref, zn_ref, bg_ref, gl_ref, bgate_ref,
               convw_ref, wa_ref, wc_ref, wo_ref, g2_ref, wgu_ref, wd_ref, g3_ref, y_ref, attention_row):
    z = z_ref[0].astype(F32)
    prev_row = jnp.where(tile_in_seq > 0, zp_ref[0, BF16_SUBLANES - 1:BF16_SUBLANES, :].astype(F32), 0.0)
    next_row = jnp.where(tile_in_seq < tiles_per_seq - 1, zn_ref[0, 0:1, :].astype(F32), 0.0)
    ridx = lax.broadcasted_iota(jnp.int32, z.shape, 0)
    z_m1 = jnp.where(ridx == 0, prev_row, pltpu.roll(z, 1, 0))
    z_p1 = jnp.where(ridx == TM_MIX - 1, next_row, pltpu.roll(z, TM_MIX - 1, 0))
    cw = convw_ref[...]
    c = bg_ref[0].astype(F32) * (z_m1 * cw[0:1] + z * cw[1:2] + z_p1 * cw[2:3])

    gates = _sigmoid(gl_ref[0] + bgate_ref[...].astype(BF16))
    pa = jnp.dot(a_ref[...], wa_ref[...], preferred_element_type=F32).astype(BF16)
    pc = jnp.dot(c.astype(BF16), wc_ref[...], preferred_element_type=F32).astype(BF16)
    merged = gates[:, :D_MODEL] * pa + gates[:, D_MODEL:] * pc
    x1 = x_ref[0] + jnp.dot(merged, wo_ref[...], preferred_element_type=F32)

    xg = (x1 * g2_ref[...]).astype(BF16)
    rstd = _rstd(x1)
    acc = x1
    for ci, f in enumerate(range(0, D_FF, FF_CHUNK)):
        gate = jnp.dot(xg, wgu_ref[:, f:f + FF_CHUNK], preferred_element_type=F32) * rstd
        up = jnp.dot(xg, wgu_ref[:, D_FF + f:D_FF + f + FF_CHUNK], preferred_element_type=F32) * rstd
        act = (gate * _sigmoid(gate) * up).astype(BF16)
        acc = acc + jnp.dot(act, wd_ref[f:f + FF_CHUNK, :], preferred_element_type=F32)
        if ci < ROWS_MIX:
            attention_row(ci)
    y_ref[0] = acc * _rstd(acc) * g3_ref[...]


def _mix_kernel(q_ref, kp_ref, km_ref, kn_ref, vp_ref, vm_ref, vn_ref, bias_ref,
                x_ref, z_ref, zp_ref, zn_ref, bg_ref, gl_ref, bgate_ref, convw_ref,
                wa_ref, wc_ref, wo_ref, g2_ref, wgu_ref, wd_ref, g3_ref,
                y_ref,
                k_scr, v_scr, s_scr, p_scr, l_scr, a_next, a_cur, *, tiles_per_seq, n_tiles):
    t = pl.program_id(0)

    @pl.when(t == 0)
    def _():
        a_next[...] = jnp.zeros_like(a_next)

    a_cur[...] = a_next[...]

    for scr, prev, main, nxt in ((k_scr, kp_ref, km_ref, kn_ref), (v_scr, vp_ref, vm_ref, vn_ref)):
        scr[0:HALO] = prev[0]
        scr[HALO:HALO + TM_MIX] = main[0]
        scr[HALO + TM_MIX:] = nxt[0]
    attn_tile = jnp.minimum(t, n_tiles - 1) % tiles_per_seq
    attention_row = functools.partial(
        _attention_row, tile_in_seq=attn_tile, rows=tiles_per_seq * ROWS_MIX, q_ref=q_ref, bias_ref=bias_ref,
        k_scr=k_scr, v_scr=v_scr, s_scr=s_scr, p_scr=p_scr, l_scr=l_scr, a_ref=a_next)

    post_tile = jnp.maximum(t - 1, 0) % tiles_per_seq
    _post_tile(post_tile, tiles_per_seq, a_cur, x_ref, z_ref, zp_ref, zn_ref, bg_ref, gl_ref, bgate_ref,
               convw_ref, wa_ref, wc_ref, wo_ref, g2_ref, wgu_ref, wd_ref, g3_ref, y_ref, attention_row)


def _mix(x, q, k, v, z, bg, gl, bias, consts):
    b, l, _ = x.shape
    tiles_per_seq = l // TM_MIX
    n_tiles = b * tiles_per_seq
    halo_per_tile = TM_MIX // HALO
    n_halo = l // HALO
    zrow_per_tile = TM_MIX // BF16_SUBLANES
    n_zrow = l // BF16_SUBLANES

    def attn_tile(t):
        ta = jnp.minimum(t, n_tiles - 1)
        return ta // tiles_per_seq, ta % tiles_per_seq

    def post_tile(t):
        tp = jnp.maximum(t - 1, 0)
        return tp // tiles_per_seq, tp % tiles_per_seq

    def a_spec(width):
        return pl.BlockSpec((1, TM_MIX, width), lambda t: (*attn_tile(t), 0))

    def p_spec(width):
        return pl.BlockSpec((1, TM_MIX, width), lambda t: (*post_tile(t), 0))

    def halo_prev(t):
        bi, g = attn_tile(t)
        return bi, jnp.maximum(g * halo_per_tile - 1, 0), 0

    def halo_next(t):
        bi, g = attn_tile(t)
        return bi, jnp.minimum((g + 1) * halo_per_tile, n_halo - 1), 0

    def z_prev(t):
        bi, g = post_tile(t)
        return bi, jnp.maximum(g * zrow_per_tile - 1, 0), 0

    def z_next(t):
        bi, g = post_tile(t)
        return bi, jnp.minimum((g + 1) * zrow_per_tile, n_zrow - 1), 0

    kv_halo = lambda index_map: pl.BlockSpec((1, HALO, D_ATTN), index_map)
    z_halo = lambda index_map: pl.BlockSpec((1, BF16_SUBLANES, D_CONV), index_map)
    kv_specs = [kv_halo(halo_prev), a_spec(D_ATTN), kv_halo(halo_next)]
    return pl.pallas_call(
        functools.partial(_mix_kernel, tiles_per_seq=tiles_per_seq, n_tiles=n_tiles),
        grid=(n_tiles + 1,),
        in_specs=[a_spec(D_ATTN)] + kv_specs + kv_specs + [_const_spec(bias.shape)]
                 + [p_spec(D_MODEL), p_spec(D_CONV), z_halo(z_prev), z_halo(z_next), p_spec(D_CONV),
                    p_spec(2 * D_MODEL)] + [_const_spec(c.shape) for c in consts],
        out_specs=p_spec(D_MODEL),
        out_shape=jax.ShapeDtypeStruct((b, l, D_MODEL), F32),
        scratch_shapes=[pltpu.VMEM((TM_MIX + 2 * HALO, D_ATTN), BF16),
                        pltpu.VMEM((TM_MIX + 2 * HALO, D_ATTN), BF16),
                        pltpu.VMEM((N_PAIRS, 2 * GRID_W, WIN_TOKENS), F32),
                        pltpu.VMEM((N_PAIRS, 2 * GRID_W, WIN_TOKENS), BF16),
                        pltpu.VMEM((N_PAIRS, 2 * GRID_W, LANES), F32),
                        pltpu.VMEM((TM_MIX, D_ATTN), BF16),
                        pltpu.VMEM((TM_MIX, D_ATTN), BF16)],
        compiler_params=pltpu.CompilerParams(
            dimension_semantics=("arbitrary",), vmem_limit_bytes=60 * 2**20),
        name="mix",
    )(q, k, k, k, v, v, v, bias, x, z, z, z, bg, gl, *consts)


def _bias_table(rpb):
    qcol = np.arange(GRID_W)[:, None]
    kcol = np.arange(GRID_W)[None, :]
    cstart = np.clip(qcol - WIN_COLS // 2, 0, GRID_W - WIN_COLS)
    valid = (kcol >= cstart) & (kcol < cstart + WIN_COLS)
    cidx = np.clip(kcol - qcol + WIN_COLS - 1, 0, 2 * WIN_COLS - 2)
    t = jnp.where(valid[None, None], rpb[:, :, cidx].astype(F32) * LOG2E, NEG_INF)
    n_ri = 2 * WIN_ROWS - 1
    full = t.transpose(0, 2, 1, 3).reshape(N_PAIRS, 2 * GRID_W, n_ri * GRID_W)
    width = BIAS_CHUNKS * LANES
    tabs = [full[:, :, par * GRID_W: par * GRID_W + width]
            .reshape(N_PAIRS, 2 * GRID_W, BIAS_CHUNKS, LANES).transpose(0, 2, 1, 3) for par in range(2)]
    return jnp.stack(tabs)


def kernel(x_prompt, x_sample, norm_mix_g, w_in, b_gate, rpb, conv_w, w_attn_branch, w_conv_branch,
           w_out, norm_ffn_g, w_ffn_in, w_ffn_down, norm_final_g):
    assert w_in.shape[0] == 1, "the final norm is fused into the single layer's last kernel"
    bias = _bias_table(rpb[0])
    w_in_b = w_in[0].astype(BF16)
    g1 = norm_mix_g[0].reshape(1, D_MODEL)
    consts = (b_gate[0].reshape(1, 2 * D_MODEL), conv_w[0], w_attn_branch[0].astype(BF16),
              w_conv_branch[0].astype(BF16), w_out[0].astype(BF16), norm_ffn_g[0].reshape(1, D_MODEL),
              w_ffn_in[0].astype(BF16), w_ffn_down[0].astype(BF16), norm_final_g.reshape(1, D_MODEL))

    def trunk(x):
        q, k, v, z, bg, gl = _inproj(x, g1, w_in_b)
        return _mix(x, q, k, v, z, bg, gl, bias, consts)

    return trunk(x_prompt), trunk(x_sample)
```

```python
import functools

import numpy as np
import jax
import jax.numpy as jnp
from jax import lax
from jax.experimental import pallas as pl
from jax.experimental.pallas import tpu as pltpu

D_MODEL = 1024
GRID_W = 64
N_HEADS = 8
HEAD_DIM = 64
D_ATTN = N_HEADS * HEAD_DIM
D_CONV = 512
WIN_ROWS = 8
WIN_COLS = 16
D_FF = 2816
RMS_EPS = 1e-6
NEG_INF = -1e30
LOG2E = 1.4426950408889634
Q_SCALE = HEAD_DIM ** -0.5 * LOG2E

LANES = 128
BF16_SUBLANES = 16
WIN_TOKENS = WIN_ROWS * GRID_W
N_PAIRS = N_HEADS // 2
BIAS_CHUNKS = (2 * WIN_ROWS - 1) * GRID_W // LANES

TM_PROJ = 1024
TM_MIX = 512
ROWS_MIX = TM_MIX // GRID_W
HALO = (WIN_ROWS // 2) * GRID_W
FF_CHUNK = 512
ROWS_PER_FF_CHUNK = 2

BF16 = jnp.bfloat16
F32 = jnp.float32


def _const_spec(shape):
    return pl.BlockSpec(shape, lambda *_: (0,) * len(shape), pipeline_mode=pl.Buffered(1))


def _sigmoid(x):
    return 0.5 * jnp.tanh(0.5 * x) + 0.5


def _rstd(x):
    return lax.rsqrt(jnp.mean(x * x, axis=-1, keepdims=True) + RMS_EPS)


def _inproj_kernel(x_ref, g_ref, w_ref, q_ref, k_ref, v_ref, z_ref, bg_ref, gl_ref):
    x = x_ref[0]
    xg = (x * g_ref[...]).astype(BF16)
    rstd = _rstd(x)

    def proj(lo, hi):
        return jnp.dot(xg, w_ref[:, lo:hi], preferred_element_type=F32) * rstd

    q_ref[0] = (proj(0, D_ATTN) * Q_SCALE).astype(BF16)
    k_ref[0] = proj(D_ATTN, 2 * D_ATTN).astype(BF16)
    v_ref[0] = proj(2 * D_ATTN, 3 * D_ATTN).astype(BF16)
    o = 3 * D_ATTN
    u = proj(o, o + D_CONV)
    bg_ref[0] = proj(o + D_CONV, o + 2 * D_CONV).astype(BF16)
    cg = proj(o + 2 * D_CONV, o + 3 * D_CONV)
    z_ref[0] = (cg * u).astype(BF16)
    o += 3 * D_CONV
    for c in range(0, 2 * D_MODEL, 1024):
        gl_ref[0, :, c:c + 1024] = proj(o + c, o + c + 1024).astype(BF16)


def _inproj(x, g, w_in):
    b, l, _ = x.shape
    d_in = w_in.shape[1]
    tok = lambda width: pl.BlockSpec((1, TM_PROJ, width), lambda bi, i: (bi, i, 0))
    out_widths = (D_ATTN, D_ATTN, D_ATTN, D_CONV, D_CONV, 2 * D_MODEL)
    return pl.pallas_call(
        _inproj_kernel,
        grid=(b, l // TM_PROJ),
        in_specs=[tok(D_MODEL), _const_spec((1, D_MODEL)), _const_spec((D_MODEL, d_in))],
        out_specs=[tok(w) for w in out_widths],
        out_shape=[jax.ShapeDtypeStruct((b, l, w), BF16) for w in out_widths],
        compiler_params=pltpu.CompilerParams(
            dimension_semantics=("parallel", "parallel"), vmem_limit_bytes=48 * 2**20),
        name="inproj",
    )(x, g, w_in)


def _attention_row(i, tile_in_seq, rows, q_ref, bias_ref, k_scr, v_scr, s_scr, p_scr, l_scr, a_ref):
    lane = lax.broadcasted_iota(jnp.int32, (GRID_W, LANES), 1)
    low_half = lane < HEAD_DIM
    row0 = tile_in_seq * ROWS_MIX
    r = row0 + i
    rs = jnp.clip(r - WIN_ROWS // 2, 0, rows - WIN_ROWS)
    shift = WIN_ROWS - 1 - (r - rs)
    parity = shift & 1
    chunk0 = shift >> 1
    k_off = pl.multiple_of((rs - row0 + WIN_ROWS // 2) * GRID_W, GRID_W)
    q_rows = slice(i * GRID_W, (i + 1) * GRID_W)
    for hp in range(N_PAIRS):
        cols = slice(hp * LANES, (hp + 1) * LANES)
        q_pair = q_ref[0, q_rows, cols]
        k_pair = k_scr[pl.ds(k_off, WIN_TOKENS), cols]
        zero = jnp.zeros_like(q_pair)
        q_both = jnp.concatenate([jnp.where(low_half, q_pair, zero),
                                  jnp.where(low_half, zero, q_pair)], axis=0)
        s = lax.dot_general(q_both, k_pair, (((1,), (1,)), ((), ())),
                            preferred_element_type=F32)
        bias = jnp.concatenate([bias_ref[parity, hp, chunk0 + j] for j in range(WIN_TOKENS // LANES)],
                               axis=1)
        s_scr[hp] = s + bias
    for hp in range(N_PAIRS):
        s = s_scr[hp]
        m = jnp.max(s, axis=-1, keepdims=True)
        p = jnp.exp2(s - m)
        l_scr[hp] = jnp.broadcast_to(jnp.sum(p, axis=-1, keepdims=True), (2 * GRID_W, LANES))
        p_scr[hp] = p.astype(BF16)
    for hp in range(N_PAIRS):
        cols = slice(hp * LANES, (hp + 1) * LANES)
        v_pair = v_scr[pl.ds(k_off, WIN_TOKENS), cols]
        o = jnp.dot(p_scr[hp], v_pair, preferred_element_type=F32) / l_scr[hp]
        a_ref[q_rows, cols] = jnp.where(low_half, o[:GRID_W], o[GRID_W:]).astype(BF16)


def _post_tile(tile_in_seq, tiles_per_seq, a_ref, x_ref, z_ref, zp_ref, zn_ref, bg_ref, gl_ref, bgate_ref,
               convw_ref, wa_ref, wc_ref, wo_ref, g2_ref, wgu_ref, wd_ref, g3_ref, y_ref, attention_row):
    z = z_ref[0].astype(F32)
    prev_row = jnp.where(tile_in_seq > 0, zp_ref[0, BF16_SUBLANES - 1:BF16_SUBLANES, :].astype(F32), 0.0)
    next_row = jnp.where(tile_in_seq < tiles_per_seq - 1, zn_ref[0, 0:1, :].astype(F32), 0.0)
    ridx = lax.broadcasted_iota(jnp.int32, z.shape, 0)
    z_m1 = jnp.where(ridx == 0, prev_row, pltpu.roll(z, 1, 0))
    z_p1 = jnp.where(ridx == TM_MIX - 1, next_row, pltpu.roll(z, TM_MIX - 1, 0))
    cw = convw_ref[...]
    c = bg_ref[0].astype(F32) * (z_m1 * cw[0:1] + z * cw[1:2] + z_p1 * cw[2:3])

    gates = _sigmoid(gl_ref[0] + bgate_ref[...].astype(BF16))
    pa = jnp.dot(a_ref[...], wa_ref[...], preferred_element_type=F32).astype(BF16)
    pc = jnp.dot(c.astype(BF16), wc_ref[...], preferred_element_type=F32).astype(BF16)
    merged = gates[:, :D_MODEL] * pa + gates[:, D_MODEL:] * pc
    x1 = x_ref[0] + jnp.dot(merged, wo_ref[...], preferred_element_type=F32)

    xg = (x1 * g2_ref[...]).astype(BF16)
    rstd = _rstd(x1)
    acc = x1
    rows_left = list(range(ROWS_MIX))
    for f in range(0, D_FF, FF_CHUNK):
        w = min(FF_CHUNK, D_FF - f)
        gate = jnp.dot(xg, wgu_ref[:, f:f + w], preferred_element_type=F32) * rstd
        up = jnp.dot(xg, wgu_ref[:, D_FF + f:D_FF + f + w], preferred_element_type=F32) * rstd
        act = (gate * _sigmoid(gate) * up).astype(BF16)
        acc = acc + jnp.dot(act, wd_ref[f:f + w, :], preferred_element_type=F32)
        for _ in range(ROWS_PER_FF_CHUNK):
            if rows_left:
                attention_row(rows_left.pop(0))
    y_ref[0] = acc * _rstd(acc) * g3_ref[...]


def _mix_kernel(q_ref, kp_ref, km_ref, kn_ref, vp_ref, vm_ref, vn_ref, bias_ref,
                x_ref, z_ref, zp_ref, zn_ref, bg_ref, gl_ref, bgate_ref, convw_ref,
                wa_ref, wc_ref, wo_ref, g2_ref, wgu_ref, wd_ref, g3_ref,
                y_ref,
                k_scr, v_scr, s_scr, p_scr, l_scr, a_next, a_cur, *, tiles_per_seq, n_tiles):
    t = pl.program_id(0)

    @pl.when(t == 0)
    def _():
        a_next[...] = jnp.zeros_like(a_next)

    a_cur[...] = a_next[...]

    for scr, prev, main, nxt in ((k_scr, kp_ref, km_ref, kn_ref), (v_scr, vp_ref, vm_ref, vn_ref)):
        scr[0:HALO] = prev[0]
        scr[HALO:HALO + TM_MIX] = main[0]
        scr[HALO + TM_MIX:] = nxt[0]
    attn_tile = jnp.minimum(t, n_tiles - 1) % tiles_per_seq
    attention_row = functools.partial(
        _attention_row, tile_in_seq=attn_tile, rows=tiles_per_seq * ROWS_MIX, q_ref=q_ref, bias_ref=bias_ref,
        k_scr=k_scr, v_scr=v_scr, s_scr=s_scr, p_scr=p_scr, l_scr=l_scr, a_ref=a_next)

    post_tile = jnp.maximum(t - 1, 0) % tiles_per_seq
    _post_tile(post_tile, tiles_per_seq, a_cur, x_ref, z_ref, zp_ref, zn_ref, bg_ref, gl_ref, bgate_ref,
               convw_ref, wa_ref, wc_ref, wo_ref, g2_ref, wgu_ref, wd_ref, g3_ref, y_ref, attention_row)


def _mix(x, q, k, v, z, bg, gl, bias, consts):
    b, l, _ = x.shape
    tiles_per_seq = l // TM_MIX
    n_tiles = b * tiles_per_seq
    halo_per_tile = TM_MIX // HALO
    n_halo = l // HALO
    zrow_per_tile = TM_MIX // BF16_SUBLANES
    n_zrow = l // BF16_SUBLANES

    def attn_tile(t):
        ta = jnp.minimum(t, n_tiles - 1)
        return ta // tiles_per_seq, ta % tiles_per_seq

    def post_tile(t):
        tp = jnp.maximum(t - 1, 0)
        return tp // tiles_per_seq, tp % tiles_per_seq

    def a_spec(width):
        return pl.BlockSpec((1, TM_MIX, width), lambda t: (*attn_tile(t), 0))

    def p_spec(width):
        return pl.BlockSpec((1, TM_MIX, width), lambda t: (*post_tile(t), 0))

    def halo_prev(t):
        bi, g = attn_tile(t)
        return bi, jnp.maximum(g * halo_per_tile - 1, 0), 0

    def halo_next(t):
        bi, g = attn_tile(t)
        return bi, jnp.minimum((g + 1) * halo_per_tile, n_halo - 1), 0

    def z_prev(t):
        bi, g = post_tile(t)
        return bi, jnp.maximum(g * zrow_per_tile - 1, 0), 0

    def z_next(t):
        bi, g = post_tile(t)
        return bi, jnp.minimum((g + 1) * zrow_per_tile, n_zrow - 1), 0

    kv_halo = lambda index_map: pl.BlockSpec((1, HALO, D_ATTN), index_map)
    z_halo = lambda index_map: pl.BlockSpec((1, BF16_SUBLANES, D_CONV), index_map)
    kv_specs = [kv_halo(halo_prev), a_spec(D_ATTN), kv_halo(halo_next)]
    return pl.pallas_call(
        functools.partial(_mix_kernel, tiles_per_seq=tiles_per_seq, n_tiles=n_tiles),
        grid=(n_tiles + 1,),
        in_specs=[a_spec(D_ATTN)] + kv_specs + kv_specs + [_const_spec(bias.shape)]
                 + [p_spec(D_MODEL), p_spec(D_CONV), z_halo(z_prev), z_halo(z_next), p_spec(D_CONV),
                    p_spec(2 * D_MODEL)] + [_const_spec(c.shape) for c in consts],
        out_specs=p_spec(D_MODEL),
        out_shape=jax.ShapeDtypeStruct((b, l, D_MODEL), F32),
        scratch_shapes=[pltpu.VMEM((TM_MIX + 2 * HALO, D_ATTN), BF16),
                        pltpu.VMEM((TM_MIX + 2 * HALO, D_ATTN), BF16),
                        pltpu.VMEM((N_PAIRS, 2 * GRID_W, WIN_TOKENS), F32),
                        pltpu.VMEM((N_PAIRS, 2 * GRID_W, WIN_TOKENS), BF16),
                        pltpu.VMEM((N_PAIRS, 2 * GRID_W, LANES), F32),
                        pltpu.VMEM((TM_MIX, D_ATTN), BF16),
                        pltpu.VMEM((TM_MIX, D_ATTN), BF16)],
        compiler_params=pltpu.CompilerParams(
            dimension_semantics=("arbitrary",), vmem_limit_bytes=60 * 2**20),
        name="mix",
    )(q, k, k, k, v, v, v, bias, x, z, z, z, bg, gl, *consts)


def _bias_table(rpb):
    qcol = np.arange(GRID_W)[:, None]
    kcol = np.arange(GRID_W)[None, :]
    cstart = np.clip(qcol - WIN_COLS // 2, 0, GRID_W - WIN_COLS)
    valid = (kcol >= cstart) & (kcol < cstart + WIN_COLS)
    n_ri = 2 * WIN_ROWS - 1
    period = 2 * GRID_W - 1
    pad = GRID_W - WIN_COLS
    v = jnp.pad(rpb.astype(F32) * LOG2E, ((0, 0), (0, 0), (pad, period - pad - (2 * WIN_COLS - 1))))
    shifted = jnp.tile(v, (1, 1, GRID_W + 1))[:, :, :GRID_W * (period + 1)]
    shifted = shifted.reshape(N_HEADS, n_ri, GRID_W, period + 1)[:, :, ::-1, :GRID_W]
    t = jnp.where(valid[None, None], shifted, NEG_INF)
    full = t.transpose(0, 2, 1, 3).reshape(N_PAIRS, 2 * GRID_W, n_ri * GRID_W)
    width = BIAS_CHUNKS * LANES
    tabs = [full[:, :, par * GRID_W: par * GRID_W + width]
            .reshape(N_PAIRS, 2 * GRID_W, BIAS_CHUNKS, LANES).transpose(0, 2, 1, 3) for par in range(2)]
    return jnp.stack(tabs)


def kernel(x_prompt, x_sample, norm_mix_g, w_in, b_gate, rpb, conv_w, w_attn_branch, w_conv_branch,
           w_out, norm_ffn_g, w_ffn_in, w_ffn_down, norm_final_g):
    assert w_in.shape[0] == 1, "the final norm is fused into the single layer's last kernel"
    bias = _bias_table(rpb[0])
    w_in_b = w_in[0].astype(BF16)
    g1 = norm_mix_g[0].reshape(1, D_MODEL)
    consts = (b_gate[0].reshape(1, 2 * D_MODEL), conv_w[0], w_attn_branch[0].astype(BF16),
              w_conv_branch[0].astype(BF16), w_out[0].astype(BF16), norm_ffn_g[0].reshape(1, D_MODEL),
              w_ffn_in[0].astype(BF16), w_ffn_down[0].astype(BF16), norm_final_g.reshape(1, D_MODEL))

    def trunk(x):
        q, k, v, z, bg, gl = _inproj(x, g1, w_in_b)
        return _mix(x, q, k, v, z, bg, gl, bias, consts)

    return trunk(x_prompt), trunk(x_sample)
```

```python
import functools

import numpy as np
import jax
import jax.numpy as jnp
from jax import lax
from jax.experimental import pallas as pl
from jax.experimental.pallas import tpu as pltpu

D_MODEL = 1024
GRID_W = 64
N_HEADS = 8
HEAD_DIM = 64
D_ATTN = N_HEADS * HEAD_DIM
D_CONV = 512
WIN_ROWS = 8
WIN_COLS = 16
D_FF = 2816
RMS_EPS = 1e-6
NEG_INF = -1e30
LOG2E = 1.4426950408889634
Q_SCALE = HEAD_DIM ** -0.5 * LOG2E

LANES = 128
BF16_SUBLANES = 16
WIN_TOKENS = WIN_ROWS * GRID_W
N_PAIRS = N_HEADS // 2
BIAS_CHUNKS = (2 * WIN_ROWS - 1) * GRID_W // LANES

TM_PROJ = 1024
TM_MIX = 512
ROWS_MIX = TM_MIX // GRID_W
HALO = (WIN_ROWS // 2) * GRID_W
FF_CHUNK = 512
ROWS_PER_FF_CHUNK = 2

BF16 = jnp.bfloat16
F32 = jnp.float32


def _const_spec(shape):
    return pl.BlockSpec(shape, lambda *_: (0,) * len(shape), pipeline_mode=pl.Buffered(1))


def _sigmoid(x):
    return 0.5 * jnp.tanh(0.5 * x) + 0.5


def _rstd(x):
    return lax.rsqrt(jnp.mean(x * x, axis=-1, keepdims=True) + RMS_EPS)


def _inproj_kernel(x_ref, g_ref, w_ref, q_ref, k_ref, v_ref, z_ref, bg_ref, gl_ref):
    x = x_ref[0]
    xg = (x * g_ref[...]).astype(BF16)
    rstd = _rstd(x)

    def proj(lo, hi):
        return jnp.dot(xg, w_ref[:, lo:hi], preferred_element_type=F32) * rstd

    q_ref[0] = (proj(0, D_ATTN) * Q_SCALE).astype(BF16)
    k_ref[0] = proj(D_ATTN, 2 * D_ATTN).astype(BF16)
    v_ref[0] = proj(2 * D_ATTN, 3 * D_ATTN).astype(BF16)
    o = 3 * D_ATTN
    u = proj(o, o + D_CONV)
    bg_ref[0] = proj(o + D_CONV, o + 2 * D_CONV).astype(BF16)
    cg = proj(o + 2 * D_CONV, o + 3 * D_CONV)
    z_ref[0] = (cg * u).astype(BF16)
    o += 3 * D_CONV
    for c in range(0, 2 * D_MODEL, 1024):
        gl_ref[0, :, c:c + 1024] = proj(o + c, o + c + 1024).astype(BF16)


def _inproj(x, g, w_in):
    b, l, _ = x.shape
    d_in = w_in.shape[1]
    tok = lambda width: pl.BlockSpec((1, TM_PROJ, width), lambda bi, i: (bi, i, 0))
    out_widths = (D_ATTN, D_ATTN, D_ATTN, D_CONV, D_CONV, 2 * D_MODEL)
    return pl.pallas_call(
        _inproj_kernel,
        grid=(b, l // TM_PROJ),
        in_specs=[tok(D_MODEL), _const_spec((1, D_MODEL)), _const_spec((D_MODEL, d_in))],
        out_specs=[tok(w) for w in out_widths],
        out_shape=[jax.ShapeDtypeStruct((b, l, w), BF16) for w in out_widths],
        compiler_params=pltpu.CompilerParams(
            dimension_semantics=("parallel", "parallel"), vmem_limit_bytes=48 * 2**20),
        name="inproj",
    )(x, g, w_in)


def _attention_row(i, tile_in_seq, rows, q_ref, bias_ref, k_scr, v_scr, s_scr, p_scr, l_scr, a_ref):
    lane = lax.broadcasted_iota(jnp.int32, (GRID_W, LANES), 1)
    low_half = lane < HEAD_DIM
    row0 = tile_in_seq * ROWS_MIX
    r = row0 + i
    rs = jnp.clip(r - WIN_ROWS // 2, 0, rows - WIN_ROWS)
    shift = WIN_ROWS - 1 - (r - rs)
    parity = shift & 1
    chunk0 = shift >> 1
    k_off = pl.multiple_of((rs - row0 + WIN_ROWS // 2) * GRID_W, GRID_W)
    q_rows = slice(i * GRID_W, (i + 1) * GRID_W)
    for hp in range(N_PAIRS):
        cols = slice(hp * LANES, (hp + 1) * LANES)
        q_pair = q_ref[0, q_rows, cols]
        k_pair = k_scr[pl.ds(k_off, WIN_TOKENS), cols]
        zero = jnp.zeros_like(q_pair)
        q_both = jnp.concatenate([jnp.where(low_half, q_pair, zero),
                                  jnp.where(low_half, zero, q_pair)], axis=0)
        s = lax.dot_general(q_both, k_pair, (((1,), (1,)), ((), ())),
                            preferred_element_type=F32)
        bias = jnp.concatenate([bias_ref[parity, hp, chunk0 + j] for j in range(WIN_TOKENS // LANES)],
                               axis=1)
        s_scr[hp] = s + bias
    for hp in range(N_PAIRS):
        s = s_scr[hp]
        m = jnp.max(s, axis=-1, keepdims=True)
        p = jnp.exp2(s - m)
        l_scr[hp] = jnp.broadcast_to(jnp.sum(p, axis=-1, keepdims=True), (2 * GRID_W, LANES))
        p_scr[hp] = p.astype(BF16)
    for hp in range(N_PAIRS):
        cols = slice(hp * LANES, (hp + 1) * LANES)
        v_pair = v_scr[pl.ds(k_off, WIN_TOKENS), cols]
        o = jnp.dot(p_scr[hp], v_pair, preferred_element_type=F32) / l_scr[hp]
        a_ref[q_rows, cols] = jnp.where(low_half, o[:GRID_W], o[GRID_W:]).astype(BF16)


def _post_tile(tile_in_seq, tiles_per_seq, a_ref, x_ref, z_ref, zp_ref, zn_ref, bg_ref, gl_ref, bgate_ref,
               convw_ref, wa_ref, wc_ref, wo_ref, g2_ref, wgu_ref, wd_ref, g3_ref, y_ref, attention_row):
    z = z_ref[0].astype(F32)
    prev_row = jnp.where(tile_in_seq > 0, zp_ref[0, BF16_SUBLANES - 1:BF16_SUBLANES, :].astype(F32), 0.0)
    next_row = jnp.where(tile_in_seq < tiles_per_seq - 1, zn_ref[0, 0:1, :].astype(F32), 0.0)
    ridx = lax.broadcasted_iota(jnp.int32, z.shape, 0)
    z_m1 = jnp.where(ridx == 0, prev_row, pltpu.roll(z, 1, 0))
    z_p1 = jnp.where(ridx == TM_MIX - 1, next_row, pltpu.roll(z, TM_MIX - 1, 0))
    cw = convw_ref[...]
    c = bg_ref[0].astype(F32) * (z_m1 * cw[0:1] + z * cw[1:2] + z_p1 * cw[2:3])

    gates = _sigmoid(gl_ref[0] + bgate_ref[...].astype(BF16))
    pa = jnp.dot(a_ref[...], wa_ref[...], preferred_element_type=F32).astype(BF16)
    pc = jnp.dot(c.astype(BF16), wc_ref[...], preferred_element_type=F32).astype(BF16)
    merged = gates[:, :D_MODEL] * pa + gates[:, D_MODEL:] * pc
    x1 = x_ref[0] + jnp.dot(merged, wo_ref[...], preferred_element_type=F32)

    xg = (x1 * g2_ref[...]).astype(BF16)
    rstd = _rstd(x1)
    acc = x1
    rows_left = list(range(ROWS_MIX))
    for f in range(0, D_FF, FF_CHUNK):
        w = min(FF_CHUNK, D_FF - f)
        gate = jnp.dot(xg, wgu_ref[:, f:f + w], preferred_element_type=F32) * rstd
        up = jnp.dot(xg, wgu_ref[:, D_FF + f:D_FF + f + w], preferred_element_type=F32) * rstd
        act = (gate * _sigmoid(gate) * up).astype(BF16)
        acc = acc + jnp.dot(act, wd_ref[f:f + w, :], preferred_element_type=F32)
        for _ in range(ROWS_PER_FF_CHUNK):
            if rows_left:
                attention_row(rows_left.pop(0))
    y_ref[0] = acc * _rstd(acc) * g3_ref[...]


def _mix_kernel(q_ref, kp_ref, km_ref, kn_ref, vp_ref, vm_ref, vn_ref, bias_ref,
                x_ref, z_ref, zp_ref, zn_ref, bg_ref, gl_ref, bgate_ref, convw_ref,
                wa_ref, wc_ref, wo_ref, g2_ref, wgu_ref, wd_ref, g3_ref,
                y_ref,
                k_scr, v_scr, s_scr, p_scr, l_scr, a_next, a_cur, *, tiles_per_seq, n_tiles):
    t = pl.program_id(0)

    def step(with_attention, with_post):
        if with_post:
            a_cur[...] = a_next[...]
        if with_attention:
            for scr, prev, main, nxt in ((k_scr, kp_ref, km_ref, kn_ref), (v_scr, vp_ref, vm_ref, vn_ref)):
                scr[0:HALO] = prev[0]
                scr[HALO:HALO + TM_MIX] = main[0]
                scr[HALO + TM_MIX:] = nxt[0]
            attention_row = functools.partial(
                _attention_row, tile_in_seq=t % tiles_per_seq, rows=tiles_per_seq * ROWS_MIX, q_ref=q_ref,
                bias_ref=bias_ref, k_scr=k_scr, v_scr=v_scr, s_scr=s_scr, p_scr=p_scr, l_scr=l_scr,
                a_ref=a_next)
        else:
            attention_row = lambda i: None
        if with_post:
            _post_tile((t - 1) % tiles_per_seq, tiles_per_seq, a_cur, x_ref, z_ref, zp_ref, zn_ref, bg_ref,
                       gl_ref, bgate_ref, convw_ref, wa_ref, wc_ref, wo_ref, g2_ref, wgu_ref, wd_ref, g3_ref,
                       y_ref, attention_row)
        else:
            for i in range(ROWS_MIX):
                attention_row(i)

    pl.when(t == 0)(functools.partial(step, True, False))
    pl.when(jnp.logical_and(t > 0, t < n_tiles))(functools.partial(step, True, True))
    pl.when(t == n_tiles)(functools.partial(step, False, True))


def _mix(x, q, k, v, z, bg, gl, bias, consts):
    b, l, _ = x.shape
    tiles_per_seq = l // TM_MIX
    n_tiles = b * tiles_per_seq
    halo_per_tile = TM_MIX // HALO
    n_halo = l // HALO
    zrow_per_tile = TM_MIX // BF16_SUBLANES
    n_zrow = l // BF16_SUBLANES

    def attn_tile(t):
        ta = jnp.minimum(t, n_tiles - 1)
        return ta // tiles_per_seq, ta % tiles_per_seq

    def post_tile(t):
        tp = jnp.maximum(t - 1, 0)
        return tp // tiles_per_seq, tp % tiles_per_seq

    def a_spec(width):
        return pl.BlockSpec((1, TM_MIX, width), lambda t: (*attn_tile(t), 0))

    def p_spec(width):
        return pl.BlockSpec((1, TM_MIX, width), lambda t: (*post_tile(t), 0))

    def halo_prev(t):
        bi, g = attn_tile(t)
        return bi, jnp.maximum(g * halo_per_tile - 1, 0), 0

    def halo_next(t):
        bi, g = attn_tile(t)
        return bi, jnp.minimum((g + 1) * halo_per_tile, n_halo - 1), 0

    def z_prev(t):
        bi, g = post_tile(t)
        return bi, jnp.maximum(g * zrow_per_tile - 1, 0), 0

    def z_next(t):
        bi, g = post_tile(t)
        return bi, jnp.minimum((g + 1) * zrow_per_tile, n_zrow - 1), 0

    kv_halo = lambda index_map: pl.BlockSpec((1, HALO, D_ATTN), index_map)
    z_halo = lambda index_map: pl.BlockSpec((1, BF16_SUBLANES, D_CONV), index_map)
    kv_specs = [kv_halo(halo_prev), a_spec(D_ATTN), kv_halo(halo_next)]
    return pl.pallas_call(
        functools.partial(_mix_kernel, tiles_per_seq=tiles_per_seq, n_tiles=n_tiles),
        grid=(n_tiles + 1,),
        in_specs=[a_spec(D_ATTN)] + kv_specs + kv_specs + [_const_spec(bias.shape)]
                 + [p_spec(D_MODEL), p_spec(D_CONV), z_halo(z_prev), z_halo(z_next), p_spec(D_CONV),
                    p_spec(2 * D_MODEL)] + [_const_spec(c.shape) for c in consts],
        out_specs=p_spec(D_MODEL),
        out_shape=jax.ShapeDtypeStruct((b, l, D_MODEL), F32),
        scratch_shapes=[pltpu.VMEM((TM_MIX + 2 * HALO, D_ATTN), BF16),
                        pltpu.VMEM((TM_MIX + 2 * HALO, D_ATTN), BF16),
                        pltpu.VMEM((N_PAIRS, 2 * GRID_W, WIN_TOKENS), F32),
                        pltpu.VMEM((N_PAIRS, 2 * GRID_W, WIN_TOKENS), BF16),
                        pltpu.VMEM((N_PAIRS, 2 * GRID_W, LANES), F32),
                        pltpu.VMEM((TM_MIX, D_ATTN), BF16),
                        pltpu.VMEM((TM_MIX, D_ATTN), BF16)],
        compiler_params=pltpu.CompilerParams(
            dimension_semantics=("arbitrary",), vmem_limit_bytes=60 * 2**20),
        name="mix",
    )(q, k, k, k, v, v, v, bias, x, z, z, z, bg, gl, *consts)


def _bias_table(rpb):
    qcol = np.arange(GRID_W)[:, None]
    kcol = np.arange(GRID_W)[None, :]
    cstart = np.clip(qcol - WIN_COLS // 2, 0, GRID_W - WIN_COLS)
    valid = (kcol >= cstart) & (kcol < cstart + WIN_COLS)
    n_ri = 2 * WIN_ROWS - 1
    period = 2 * GRID_W - 1
    pad = GRID_W - WIN_COLS
    v = jnp.pad(rpb.astype(F32) * LOG2E, ((0, 0), (0, 0), (pad, period - pad - (2 * WIN_COLS - 1))))
    shifted = jnp.tile(v, (1, 1, GRID_W + 1))[:, :, :GRID_W * (period + 1)]
    shifted = shifted.reshape(N_HEADS, n_ri, GRID_W, period + 1)[:, :, ::-1, :GRID_W]
    t = jnp.where(valid[None, None], shifted, NEG_INF)
    full = t.transpose(0, 2, 1, 3).reshape(N_PAIRS, 2 * GRID_W, n_ri * GRID_W)
    width = BIAS_CHUNKS * LANES
    tabs = [full[:, :, par * GRID_W: par * GRID_W + width]
            .reshape(N_PAIRS, 2 * GRID_W, BIAS_CHUNKS, LANES).transpose(0, 2, 1, 3) for par in range(2)]
    return jnp.stack(tabs)


def kernel(x_prompt, x_sample, norm_mix_g, w_in, b_gate, rpb, conv_w, w_attn_branch, w_conv_branch,
           w_out, norm_ffn_g, w_ffn_in, w_ffn_down, norm_final_g):
    assert w_in.shape[0] == 1, "the final norm is fused into the single layer's last kernel"
    bias = _bias_table(rpb[0])
    w_in_b = w_in[0].astype(BF16)
    g1 = norm_mix_g[0].reshape(1, D_MODEL)
    consts = (b_gate[0].reshape(1, 2 * D_MODEL), conv_w[0], w_attn_branch[0].astype(BF16),
              w_conv_branch[0].astype(BF16), w_out[0].astype(BF16), norm_ffn_g[0].reshape(1, D_MODEL),
              w_ffn_in[0].astype(BF16), w_ffn_down[0].astype(BF16), norm_final_g.reshape(1, D_MODEL))

    def trunk(x):
        q, k, v, z, bg, gl = _inproj(x, g1, w_in_b)
        return _mix(x, q, k, v, z, bg, gl, bias, consts)

    return trunk(x_prompt), trunk(x_sample)
```

```python
import functools

import numpy as np
import jax
import jax.numpy as jnp
from jax import lax
from jax.experimental import pallas as pl
from jax.experimental.pallas import tpu as pltpu

D_MODEL = 1024
GRID_W = 64
N_HEADS = 8
HEAD_DIM = 64
D_ATTN = N_HEADS * HEAD_DIM
D_CONV = 512
WIN_ROWS = 8
WIN_COLS = 16
D_FF = 2816
RMS_EPS = 1e-6
NEG_INF = -1e30
LOG2E = 1.4426950408889634
Q_SCALE = HEAD_DIM ** -0.5 * LOG2E

LANES = 128
BF16_SUBLANES = 16
WIN_TOKENS = WIN_ROWS * GRID_W
N_PAIRS = N_HEADS // 2
BIAS_CHUNKS = (2 * WIN_ROWS - 1) * GRID_W // LANES

TM_PROJ = 1024
TM_MIX = 512
ROWS_MIX = TM_MIX // GRID_W
HALO = (WIN_ROWS // 2) * GRID_W
KV_WINDOW = TM_MIX + 2 * HALO
FF_CHUNK = 512
ROWS_PER_FF_CHUNK = 2

BF16 = jnp.bfloat16
F32 = jnp.float32


def _const_spec(shape):
    return pl.BlockSpec(shape, lambda *_: (0,) * len(shape), pipeline_mode=pl.Buffered(1))


def _sigmoid(x):
    return 0.5 * jnp.tanh(0.5 * x) + 0.5


def _rstd(x):
    return lax.rsqrt(jnp.mean(x * x, axis=-1, keepdims=True) + RMS_EPS)


def _inproj_kernel(x_ref, g_ref, w_ref, q_ref, k_ref, v_ref, z_ref, bg_ref, gl_ref):
    x = x_ref[0]
    xg = (x * g_ref[...]).astype(BF16)
    rstd = _rstd(x)

    def proj(lo, hi):
        return jnp.dot(xg, w_ref[:, lo:hi], preferred_element_type=F32) * rstd

    q_ref[0] = (proj(0, D_ATTN) * Q_SCALE).astype(BF16)
    k_ref[0] = proj(D_ATTN, 2 * D_ATTN).astype(BF16)
    v_ref[0] = proj(2 * D_ATTN, 3 * D_ATTN).astype(BF16)
    o = 3 * D_ATTN
    u = proj(o, o + D_CONV)
    bg_ref[0] = proj(o + D_CONV, o + 2 * D_CONV).astype(BF16)
    cg = proj(o + 2 * D_CONV, o + 3 * D_CONV)
    z_ref[0] = (cg * u).astype(BF16)
    o += 3 * D_CONV
    for c in range(0, 2 * D_MODEL, 1024):
        gl_ref[0, :, c:c + 1024] = proj(o + c, o + c + 1024).astype(BF16)


def _inproj(x, g, w_in):
    b, l, _ = x.shape
    d_in = w_in.shape[1]
    tok = lambda width: pl.BlockSpec((1, TM_PROJ, width), lambda bi, i: (bi, i, 0))
    out_widths = (D_ATTN, D_ATTN, D_ATTN, D_CONV, D_CONV, 2 * D_MODEL)
    return pl.pallas_call(
        _inproj_kernel,
        grid=(b, l // TM_PROJ),
        in_specs=[tok(D_MODEL), _const_spec((1, D_MODEL)), _const_spec((D_MODEL, d_in))],
        out_specs=[tok(w) for w in out_widths],
        out_shape=[jax.ShapeDtypeStruct((b, l, w), BF16) for w in out_widths],
        compiler_params=pltpu.CompilerParams(
            dimension_semantics=("parallel", "parallel"), vmem_limit_bytes=48 * 2**20),
        name="inproj",
    )(x, g, w_in)


def _kv_window_start(tile_in_seq, seq_len):
    return jnp.clip(tile_in_seq * TM_MIX - HALO, 0, seq_len - KV_WINDOW)


def _attention_row(i, tile_in_seq, rows, q_ref, bias_ref, k_ref, v_ref, s_scr, p_scr, l_scr, a_ref):
    lane = lax.broadcasted_iota(jnp.int32, (GRID_W, LANES), 1)
    low_half = lane < HEAD_DIM
    row0 = tile_in_seq * ROWS_MIX
    r = row0 + i
    rs = jnp.clip(r - WIN_ROWS // 2, 0, rows - WIN_ROWS)
    shift = WIN_ROWS - 1 - (r - rs)
    parity = shift & 1
    chunk0 = shift >> 1
    k_off = pl.multiple_of(rs * GRID_W - _kv_window_start(tile_in_seq, rows * GRID_W), GRID_W)
    q_rows = slice(i * GRID_W, (i + 1) * GRID_W)
    for hp in range(N_PAIRS):
        cols = slice(hp * LANES, (hp + 1) * LANES)
        q_pair = q_ref[0, q_rows, cols]
        k_pair = k_ref[0, pl.ds(k_off, WIN_TOKENS), cols]
        zero = jnp.zeros_like(q_pair)
        q_both = jnp.concatenate([jnp.where(low_half, q_pair, zero),
                                  jnp.where(low_half, zero, q_pair)], axis=0)
        s = lax.dot_general(q_both, k_pair, (((1,), (1,)), ((), ())),
                            preferred_element_type=F32)
        bias = jnp.concatenate([bias_ref[parity, hp, chunk0 + j] for j in range(WIN_TOKENS // LANES)],
                               axis=1)
        s_scr[hp] = s + bias
    for hp in range(N_PAIRS):
        s = s_scr[hp]
        m = jnp.max(s, axis=-1, keepdims=True)
        p = jnp.exp2(s - m)
        l_scr[hp] = jnp.broadcast_to(jnp.sum(p, axis=-1, keepdims=True), (2 * GRID_W, LANES))
        p_scr[hp] = p.astype(BF16)
    for hp in range(N_PAIRS):
        cols = slice(hp * LANES, (hp + 1) * LANES)
        v_pair = v_ref[0, pl.ds(k_off, WIN_TOKENS), cols]
        o = jnp.dot(p_scr[hp], v_pair, preferred_element_type=F32) / l_scr[hp]
        a_ref[q_rows, cols] = jnp.where(low_half, o[:GRID_W], o[GRID_W:]).astype(BF16)


def _post_tile(tile_in_seq, tiles_per_seq, a_ref, x_ref, z_ref, zp_ref, zn_ref, bg_ref, gl_ref, bgate_ref,
               convw_ref, wa_ref, wc_ref, wo_ref, g2_ref, wgu_ref, wd_ref, g3_ref, y_ref, attention_row):
    z = z_ref[0].astype(F32)
    prev_row = jnp.where(tile_in_seq > 0, zp_ref[0, BF16_SUBLANES - 1:BF16_SUBLANES, :].astype(F32), 0.0)
    next_row = jnp.where(tile_in_seq < tiles_per_seq - 1, zn_ref[0, 0:1, :].astype(F32), 0.0)
    ridx = lax.broadcasted_iota(jnp.int32, z.shape, 0)
    z_m1 = jnp.where(ridx == 0, prev_row, pltpu.roll(z, 1, 0))
    z_p1 = jnp.where(ridx == TM_MIX - 1, next_row, pltpu.roll(z, TM_MIX - 1, 0))
    cw = convw_ref[...]
    c = bg_ref[0].astype(F32) * (z_m1 * cw[0:1] + z * cw[1:2] + z_p1 * cw[2:3])

    gates = _sigmoid(gl_ref[0] + bgate_ref[...].astype(BF16))
    pa = jnp.dot(a_ref[...], wa_ref[...], preferred_element_type=F32).astype(BF16)
    pc = jnp.dot(c.astype(BF16), wc_ref[...], preferred_element_type=F32).astype(BF16)
    merged = gates[:, :D_MODEL] * pa + gates[:, D_MODEL:] * pc
    x1 = x_ref[0] + jnp.dot(merged, wo_ref[...], preferred_element_type=F32)

    xg = (x1 * g2_ref[...]).astype(BF16)
    rstd = _rstd(x1)
    acc = x1
    rows_left = list(range(ROWS_MIX))
    for f in range(0, D_FF, FF_CHUNK):
        w = min(FF_CHUNK, D_FF - f)
        gate = jnp.dot(xg, wgu_ref[:, f:f + w], preferred_element_type=F32) * rstd
        up = jnp.dot(xg, wgu_ref[:, D_FF + f:D_FF + f + w], preferred_element_type=F32) * rstd
        act = (gate * _sigmoid(gate) * up).astype(BF16)
        acc = acc + jnp.dot(act, wd_ref[f:f + w, :], preferred_element_type=F32)
        for _ in range(ROWS_PER_FF_CHUNK):
            if rows_left:
                attention_row(rows_left.pop(0))
    y_ref[0] = acc * _rstd(acc) * g3_ref[...]


def _mix_kernel(q_ref, k_ref, v_ref, bias_ref,
                x_ref, z_ref, zp_ref, zn_ref, bg_ref, gl_ref, bgate_ref, convw_ref,
                wa_ref, wc_ref, wo_ref, g2_ref, wgu_ref, wd_ref, g3_ref,
                y_ref,
                s_scr, p_scr, l_scr, a_next, a_cur, *, tiles_per_seq, n_tiles):
    t = pl.program_id(0)

    @pl.when(t == 0)
    def _():
        a_next[...] = jnp.zeros_like(a_next)

    a_cur[...] = a_next[...]

    attn_tile = jnp.minimum(t, n_tiles - 1) % tiles_per_seq
    attention_row = functools.partial(
        _attention_row, tile_in_seq=attn_tile, rows=tiles_per_seq * ROWS_MIX, q_ref=q_ref, bias_ref=bias_ref,
        k_ref=k_ref, v_ref=v_ref, s_scr=s_scr, p_scr=p_scr, l_scr=l_scr, a_ref=a_next)

    post_tile = jnp.maximum(t - 1, 0) % tiles_per_seq
    _post_tile(post_tile, tiles_per_seq, a_cur, x_ref, z_ref, zp_ref, zn_ref, bg_ref, gl_ref, bgate_ref,
               convw_ref, wa_ref, wc_ref, wo_ref, g2_ref, wgu_ref, wd_ref, g3_ref, y_ref, attention_row)


def _mix(x, q, k, v, z, bg, gl, bias, consts):
    b, l, _ = x.shape
    tiles_per_seq = l // TM_MIX
    n_tiles = b * tiles_per_seq
    zrow_per_tile = TM_MIX // BF16_SUBLANES
    n_zrow = l // BF16_SUBLANES

    def attn_tile(t):
        ta = jnp.minimum(t, n_tiles - 1)
        return ta // tiles_per_seq, ta % tiles_per_seq

    def post_tile(t):
        tp = jnp.maximum(t - 1, 0)
        return tp // tiles_per_seq, tp % tiles_per_seq

    def a_spec(width):
        return pl.BlockSpec((1, TM_MIX, width), lambda t: (*attn_tile(t), 0))

    def p_spec(width):
        return pl.BlockSpec((1, TM_MIX, width), lambda t: (*post_tile(t), 0))

    def kv_window(t):
        bi, g = attn_tile(t)
        return bi, pl.multiple_of(_kv_window_start(g, l), HALO), 0

    def z_prev(t):
        bi, g = post_tile(t)
        return bi, jnp.maximum(g * zrow_per_tile - 1, 0), 0

    def z_next(t):
        bi, g = post_tile(t)
        return bi, jnp.minimum((g + 1) * zrow_per_tile, n_zrow - 1), 0

    z_halo = lambda index_map: pl.BlockSpec((1, BF16_SUBLANES, D_CONV), index_map)
    kv_spec = pl.BlockSpec((pl.Element(1), pl.Element(KV_WINDOW), pl.Element(D_ATTN)), kv_window)
    return pl.pallas_call(
        functools.partial(_mix_kernel, tiles_per_seq=tiles_per_seq, n_tiles=n_tiles),
        grid=(n_tiles + 1,),
        in_specs=[a_spec(D_ATTN), kv_spec, kv_spec, _const_spec(bias.shape)]
                 + [p_spec(D_MODEL), p_spec(D_CONV), z_halo(z_prev), z_halo(z_next), p_spec(D_CONV),
                    p_spec(2 * D_MODEL)] + [_const_spec(c.shape) for c in consts],
        out_specs=p_spec(D_MODEL),
        out_shape=jax.ShapeDtypeStruct((b, l, D_MODEL), F32),
        scratch_shapes=[pltpu.VMEM((N_PAIRS, 2 * GRID_W, WIN_TOKENS), F32),
                        pltpu.VMEM((N_PAIRS, 2 * GRID_W, WIN_TOKENS), BF16),
                        pltpu.VMEM((N_PAIRS, 2 * GRID_W, LANES), F32),
                        pltpu.VMEM((TM_MIX, D_ATTN), BF16),
                        pltpu.VMEM((TM_MIX, D_ATTN), BF16)],
        compiler_params=pltpu.CompilerParams(
            dimension_semantics=("arbitrary",), vmem_limit_bytes=60 * 2**20),
        name="mix",
    )(q, k, v, bias, x, z, z, z, bg, gl, *consts)


def _bias_table(rpb):
    qcol = np.arange(GRID_W)[:, None]
    kcol = np.arange(GRID_W)[None, :]
    cstart = np.clip(qcol - WIN_COLS // 2, 0, GRID_W - WIN_COLS)
    valid = (kcol >= cstart) & (kcol < cstart + WIN_COLS)
    n_ri = 2 * WIN_ROWS - 1
    period = 2 * GRID_W - 1
    pad = GRID_W - WIN_COLS
    v = jnp.pad(rpb.astype(F32) * LOG2E, ((0, 0), (0, 0), (pad, period - pad - (2 * WIN_COLS - 1))))
    shifted = jnp.tile(v, (1, 1, GRID_W + 1))[:, :, :GRID_W * (period + 1)]
    shifted = shifted.reshape(N_HEADS, n_ri, GRID_W, period + 1)[:, :, ::-1, :GRID_W]
    t = jnp.where(valid[None, None], shifted, NEG_INF)
    full = t.transpose(0, 2, 1, 3).reshape(N_PAIRS, 2 * GRID_W, n_ri * GRID_W)
    width = BIAS_CHUNKS * LANES
    tabs = [full[:, :, par * GRID_W: par * GRID_W + width]
            .reshape(N_PAIRS, 2 * GRID_W, BIAS_CHUNKS, LANES).transpose(0, 2, 1, 3) for par in range(2)]
    return jnp.stack(tabs)


def kernel(x_prompt, x_sample, norm_mix_g, w_in, b_gate, rpb, conv_w, w_attn_branch, w_conv_branch,
           w_out, norm_ffn_g, w_ffn_in, w_ffn_down, norm_final_g):
    assert w_in.shape[0] == 1, "the final norm is fused into the single layer's last kernel"
    bias = _bias_table(rpb[0])
    w_in_b = w_in[0].astype(BF16)
    g1 = norm_mix_g[0].reshape(1, D_MODEL)
    consts = (b_gate[0].reshape(1, 2 * D_MODEL), conv_w[0], w_attn_branch[0].astype(BF16),
              w_conv_branch[0].astype(BF16), w_out[0].astype(BF16), norm_ffn_g[0].reshape(1, D_MODEL),
              w_ffn_in[0].astype(BF16), w_ffn_down[0].astype(BF16), norm_final_g.reshape(1, D_MODEL))

    def trunk(x):
        q, k, v, z, bg, gl = _inproj(x, g1, w_in_b)
        return _mix(x, q, k, v, z, bg, gl, bias, consts)

    return trunk(x_prompt), trunk(x_sample)
```

```python
import functools

import numpy as np
import jax
import jax.numpy as jnp
from jax import lax
from jax.experimental import pallas as pl
from jax.experimental.pallas import tpu as pltpu

D_MODEL = 1024
GRID_W = 64
N_HEADS = 8
HEAD_DIM = 64
D_ATTN = N_HEADS * HEAD_DIM
D_CONV = 512
WIN_ROWS = 8
WIN_COLS = 16
D_FF = 2816
RMS_EPS = 1e-6
NEG_INF = -1e30
LOG2E = 1.4426950408889634
Q_SCALE = HEAD_DIM ** -0.5 * LOG2E

LANES = 128
BF16_SUBLANES = 16
WIN_TOKENS = WIN_ROWS * GRID_W
N_PAIRS = N_HEADS // 2
BIAS_CHUNKS = (2 * WIN_ROWS - 1) * GRID_W // LANES

TM_PROJ = 1024
TM_MIX = 512
ROWS_MIX = TM_MIX // GRID_W
HALO = (WIN_ROWS // 2) * GRID_W
KV_WINDOW = TM_MIX + 2 * HALO
FF_CHUNK = 512
ROWS_PER_FF_CHUNK = 2

BF16 = jnp.bfloat16
F32 = jnp.float32


VMEM_BYTES_V7X = 64 * 2**20


def _const_spec(shape):
    return pl.BlockSpec(shape, lambda *_: (0,) * len(shape), pipeline_mode=pl.Buffered(1))


def _nbytes(shape, dtype):
    return int(np.prod(shape)) * jnp.dtype(dtype).itemsize


def _vmem_limit(resident, streamed, scratch, live_f32_tiles, tile_shape):
    need = (sum(_nbytes(*b) for b in resident) + 2 * sum(_nbytes(*b) for b in streamed)
            + sum(_nbytes(*b) for b in scratch) + live_f32_tiles * _nbytes(tile_shape, F32))
    assert need <= VMEM_BYTES_V7X, f"estimated VMEM need {need} exceeds the v7x TensorCore's VMEM"
    return need


def _sigmoid(x):
    return 0.5 * jnp.tanh(0.5 * x) + 0.5


def _rstd(x):
    return lax.rsqrt(jnp.mean(x * x, axis=-1, keepdims=True) + RMS_EPS)


def _inproj_kernel(x_ref, g_ref, w_ref, q_ref, k_ref, v_ref, z_ref, bg_ref, gl_ref):
    x = x_ref[0]
    xg = (x * g_ref[...]).astype(BF16)
    rstd = _rstd(x)

    def proj(lo, hi):
        return jnp.dot(xg, w_ref[:, lo:hi], preferred_element_type=F32) * rstd

    q_ref[0] = (proj(0, D_ATTN) * Q_SCALE).astype(BF16)
    k_ref[0] = proj(D_ATTN, 2 * D_ATTN).astype(BF16)
    v_ref[0] = proj(2 * D_ATTN, 3 * D_ATTN).astype(BF16)
    o = 3 * D_ATTN
    u = proj(o, o + D_CONV)
    bg_ref[0] = proj(o + D_CONV, o + 2 * D_CONV).astype(BF16)
    cg = proj(o + 2 * D_CONV, o + 3 * D_CONV)
    z_ref[0] = (cg * u).astype(BF16)
    o += 3 * D_CONV
    for c in range(0, 2 * D_MODEL, 1024):
        gl_ref[0, :, c:c + 1024] = proj(o + c, o + c + 1024).astype(BF16)


def _inproj(x, g, w_in):
    b, l, _ = x.shape
    d_in = w_in.shape[1]
    tok = lambda width: pl.BlockSpec((1, TM_PROJ, width), lambda bi, i: (bi, i, 0))
    out_widths = (D_ATTN, D_ATTN, D_ATTN, D_CONV, D_CONV, 2 * D_MODEL)
    vmem_limit = _vmem_limit(
        resident=[((1, D_MODEL), F32), ((D_MODEL, d_in), BF16)],
        streamed=[((TM_PROJ, D_MODEL), F32)] + [((TM_PROJ, w), BF16) for w in out_widths],
        scratch=[], live_f32_tiles=3, tile_shape=(TM_PROJ, D_MODEL))
    return pl.pallas_call(
        _inproj_kernel,
        grid=(b, l // TM_PROJ),
        in_specs=[tok(D_MODEL), _const_spec((1, D_MODEL)), _const_spec((D_MODEL, d_in))],
        out_specs=[tok(w) for w in out_widths],
        out_shape=[jax.ShapeDtypeStruct((b, l, w), BF16) for w in out_widths],
        compiler_params=pltpu.CompilerParams(
            dimension_semantics=("parallel", "parallel"), vmem_limit_bytes=vmem_limit),
        name="inproj",
    )(x, g, w_in)


def _kv_window_start(tile_in_seq, seq_len):
    return jnp.clip(tile_in_seq * TM_MIX - HALO, 0, seq_len - KV_WINDOW)


def _attention_row(i, tile_in_seq, rows, q_ref, bias_ref, k_ref, v_ref, s_scr, p_scr, l_scr, a_ref):
    lane = lax.broadcasted_iota(jnp.int32, (GRID_W, LANES), 1)
    low_half = lane < HEAD_DIM
    row0 = tile_in_seq * ROWS_MIX
    r = row0 + i
    rs = jnp.clip(r - WIN_ROWS // 2, 0, rows - WIN_ROWS)
    shift = WIN_ROWS - 1 - (r - rs)
    parity = shift & 1
    chunk0 = shift >> 1
    k_off = pl.multiple_of(rs * GRID_W - _kv_window_start(tile_in_seq, rows * GRID_W), GRID_W)
    q_rows = slice(i * GRID_W, (i + 1) * GRID_W)
    for hp in range(N_PAIRS):
        cols = slice(hp * LANES, (hp + 1) * LANES)
        q_pair = q_ref[0, q_rows, cols]
        k_pair = k_ref[0, pl.ds(k_off, WIN_TOKENS), cols]
        zero = jnp.zeros_like(q_pair)
        q_both = jnp.concatenate([jnp.where(low_half, q_pair, zero),
                                  jnp.where(low_half, zero, q_pair)], axis=0)
        s = lax.dot_general(q_both, k_pair, (((1,), (1,)), ((), ())),
                            preferred_element_type=F32)
        bias = jnp.concatenate([bias_ref[parity, hp, chunk0 + j] for j in range(WIN_TOKENS // LANES)],
                               axis=1)
        s_scr[hp] = s + bias
    for hp in range(N_PAIRS):
        s = s_scr[hp]
        m = jnp.max(s, axis=-1, keepdims=True)
        p = jnp.exp2(s - m)
        l_scr[hp] = jnp.broadcast_to(jnp.sum(p, axis=-1, keepdims=True), (2 * GRID_W, LANES))
        p_scr[hp] = p.astype(BF16)
    for hp in range(N_PAIRS):
        cols = slice(hp * LANES, (hp + 1) * LANES)
        v_pair = v_ref[0, pl.ds(k_off, WIN_TOKENS), cols]
        o = jnp.dot(p_scr[hp], v_pair, preferred_element_type=F32) / l_scr[hp]
        a_ref[q_rows, cols] = jnp.where(low_half, o[:GRID_W], o[GRID_W:]).astype(BF16)


def _post_tile(tile_in_seq, tiles_per_seq, a_ref, x_ref, z_ref, zp_ref, zn_ref, bg_ref, gl_ref, bgate_ref,
               convw_ref, wa_ref, wc_ref, wo_ref, g2_ref, wgu_ref, wd_ref, g3_ref, y_ref, attention_row):
    z = z_ref[0].astype(F32)
    prev_row = jnp.where(tile_in_seq > 0, zp_ref[0, BF16_SUBLANES - 1:BF16_SUBLANES, :].astype(F32), 0.0)
    next_row = jnp.where(tile_in_seq < tiles_per_seq - 1, zn_ref[0, 0:1, :].astype(F32), 0.0)
    ridx = lax.broadcasted_iota(jnp.int32, z.shape, 0)
    z_m1 = jnp.where(ridx == 0, prev_row, pltpu.roll(z, 1, 0))
    z_p1 = jnp.where(ridx == TM_MIX - 1, next_row, pltpu.roll(z, TM_MIX - 1, 0))
    cw = convw_ref[...]
    c = bg_ref[0].astype(F32) * (z_m1 * cw[0:1] + z * cw[1:2] + z_p1 * cw[2:3])

    gates = _sigmoid(gl_ref[0] + bgate_ref[...].astype(BF16))
    pa = jnp.dot(a_ref[...], wa_ref[...], preferred_element_type=F32).astype(BF16)
    pc = jnp.dot(c.astype(BF16), wc_ref[...], preferred_element_type=F32).astype(BF16)
    merged = gates[:, :D_MODEL] * pa + gates[:, D_MODEL:] * pc
    x1 = x_ref[0] + jnp.dot(merged, wo_ref[...], preferred_element_type=F32)

    xg = (x1 * g2_ref[...]).astype(BF16)
    rstd = _rstd(x1)
    acc = x1
    rows_left = list(range(ROWS_MIX))
    for f in range(0, D_FF, FF_CHUNK):
        w = min(FF_CHUNK, D_FF - f)
        gate = jnp.dot(xg, wgu_ref[:, f:f + w], preferred_element_type=F32) * rstd
        up = jnp.dot(xg, wgu_ref[:, D_FF + f:D_FF + f + w], preferred_element_type=F32) * rstd
        act = (gate * _sigmoid(gate) * up).astype(BF16)
        acc = acc + jnp.dot(act, wd_ref[f:f + w, :], preferred_element_type=F32)
        for _ in range(ROWS_PER_FF_CHUNK):
            if rows_left:
                attention_row(rows_left.pop(0))
    y_ref[0] = acc * _rstd(acc) * g3_ref[...]


def _mix_kernel(q_ref, k_ref, v_ref, bias_ref,
                x_ref, z_ref, zp_ref, zn_ref, bg_ref, gl_ref, bgate_ref, convw_ref,
                wa_ref, wc_ref, wo_ref, g2_ref, wgu_ref, wd_ref, g3_ref,
                y_ref,
                s_scr, p_scr, l_scr, a_scr, *, tiles_per_seq, n_tiles):
    t = pl.program_id(0)

    @pl.when(t == 0)
    def _():
        a_scr[...] = jnp.zeros_like(a_scr)

    attn_tile = jnp.minimum(t, n_tiles - 1) % tiles_per_seq
    attention_row = functools.partial(
        _attention_row, tile_in_seq=attn_tile, rows=tiles_per_seq * ROWS_MIX, q_ref=q_ref, bias_ref=bias_ref,
        k_ref=k_ref, v_ref=v_ref, s_scr=s_scr, p_scr=p_scr, l_scr=l_scr, a_ref=a_scr)

    post_tile = jnp.maximum(t - 1, 0) % tiles_per_seq
    _post_tile(post_tile, tiles_per_seq, a_scr, x_ref, z_ref, zp_ref, zn_ref, bg_ref, gl_ref, bgate_ref,
               convw_ref, wa_ref, wc_ref, wo_ref, g2_ref, wgu_ref, wd_ref, g3_ref, y_ref, attention_row)


def _mix(x, q, k, v, z, bg, gl, bias, consts):
    b, l, _ = x.shape
    tiles_per_seq = l // TM_MIX
    n_tiles = b * tiles_per_seq
    zrow_per_tile = TM_MIX // BF16_SUBLANES
    n_zrow = l // BF16_SUBLANES

    def attn_tile(t):
        ta = jnp.minimum(t, n_tiles - 1)
        return ta // tiles_per_seq, ta % tiles_per_seq

    def post_tile(t):
        tp = jnp.maximum(t - 1, 0)
        return tp // tiles_per_seq, tp % tiles_per_seq

    def a_spec(width):
        return pl.BlockSpec((1, TM_MIX, width), lambda t: (*attn_tile(t), 0))

    def p_spec(width):
        return pl.BlockSpec((1, TM_MIX, width), lambda t: (*post_tile(t), 0))

    def kv_window(t):
        bi, g = attn_tile(t)
        return bi, pl.multiple_of(_kv_window_start(g, l), HALO), 0

    def z_prev(t):
        bi, g = post_tile(t)
        return bi, jnp.maximum(g * zrow_per_tile - 1, 0), 0

    def z_next(t):
        bi, g = post_tile(t)
        return bi, jnp.minimum((g + 1) * zrow_per_tile, n_zrow - 1), 0

    z_halo = lambda index_map: pl.BlockSpec((1, BF16_SUBLANES, D_CONV), index_map)
    kv_spec = pl.BlockSpec((pl.Element(1), pl.Element(KV_WINDOW), pl.Element(D_ATTN)), kv_window)
    scratch = [((N_PAIRS, 2 * GRID_W, WIN_TOKENS), F32),
               ((N_PAIRS, 2 * GRID_W, WIN_TOKENS), BF16),
               ((N_PAIRS, 2 * GRID_W, LANES), F32),
               ((TM_MIX, D_ATTN), BF16)]
    vmem_limit = _vmem_limit(
        resident=[(bias.shape, F32)] + [(c.shape, c.dtype) for c in consts],
        streamed=[((TM_MIX, D_ATTN), BF16), ((KV_WINDOW, D_ATTN), BF16), ((KV_WINDOW, D_ATTN), BF16),
                  ((TM_MIX, D_MODEL), F32), ((TM_MIX, D_CONV), BF16), ((2 * BF16_SUBLANES, D_CONV), BF16),
                  ((TM_MIX, D_CONV), BF16), ((TM_MIX, 2 * D_MODEL), BF16), ((TM_MIX, D_MODEL), F32)],
        scratch=scratch, live_f32_tiles=6, tile_shape=(TM_MIX, D_MODEL))
    return pl.pallas_call(
        functools.partial(_mix_kernel, tiles_per_seq=tiles_per_seq, n_tiles=n_tiles),
        grid=(n_tiles + 1,),
        in_specs=[a_spec(D_ATTN), kv_spec, kv_spec, _const_spec(bias.shape)]
                 + [p_spec(D_MODEL), p_spec(D_CONV), z_halo(z_prev), z_halo(z_next), p_spec(D_CONV),
                    p_spec(2 * D_MODEL)] + [_const_spec(c.shape) for c in consts],
        out_specs=p_spec(D_MODEL),
        out_shape=jax.ShapeDtypeStruct((b, l, D_MODEL), F32),
        scratch_shapes=[pltpu.VMEM(shape, dtype) for shape, dtype in scratch],
        compiler_params=pltpu.CompilerParams(
            dimension_semantics=("arbitrary",), vmem_limit_bytes=vmem_limit),
        name="mix",
    )(q, k, v, bias, x, z, z, z, bg, gl, *consts)


def _bias_table(rpb):
    qcol = np.arange(GRID_W)[:, None]
    kcol = np.arange(GRID_W)[None, :]
    cstart = np.clip(qcol - WIN_COLS // 2, 0, GRID_W - WIN_COLS)
    valid = (kcol >= cstart) & (kcol < cstart + WIN_COLS)
    n_ri = 2 * WIN_ROWS - 1
    period = 2 * GRID_W - 1
    pad = GRID_W - WIN_COLS
    v = jnp.pad(rpb.astype(F32) * LOG2E, ((0, 0), (0, 0), (pad, period - pad - (2 * WIN_COLS - 1))))
    shifted = jnp.tile(v, (1, 1, GRID_W + 1))[:, :, :GRID_W * (period + 1)]
    shifted = shifted.reshape(N_HEADS, n_ri, GRID_W, period + 1)[:, :, ::-1, :GRID_W]
    t = jnp.where(valid[None, None], shifted, NEG_INF)
    full = t.transpose(0, 2, 1, 3).reshape(N_PAIRS, 2 * GRID_W, n_ri * GRID_W)
    width = BIAS_CHUNKS * LANES
    tabs = [full[:, :, par * GRID_W: par * GRID_W + width]
            .reshape(N_PAIRS, 2 * GRID_W, BIAS_CHUNKS, LANES).transpose(0, 2, 1, 3) for par in range(2)]
    return jnp.stack(tabs)


def kernel(x_prompt, x_sample, norm_mix_g, w_in, b_gate, rpb, conv_w, w_attn_branch, w_conv_branch,
           w_out, norm_ffn_g, w_ffn_in, w_ffn_down, norm_final_g):
    assert w_in.shape[0] == 1, "the final norm is fused into the single layer's last kernel"
    bias = _bias_table(rpb[0])
    w_in_b = w_in[0].astype(BF16)
    g1 = norm_mix_g[0].reshape(1, D_MODEL)
    consts = (b_gate[0].reshape(1, 2 * D_MODEL), conv_w[0], w_attn_branch[0].astype(BF16),
              w_conv_branch[0].astype(BF16), w_out[0].astype(BF16), norm_ffn_g[0].reshape(1, D_MODEL),
              w_ffn_in[0].astype(BF16), w_ffn_down[0].astype(BF16), norm_final_g.reshape(1, D_MODEL))

    def trunk(x):
        q, k, v, z, bg, gl = _inproj(x, g1, w_in_b)
        return _mix(x, q, k, v, z, bg, gl, bias, consts)

    return trunk(x_prompt), trunk(x_sample)
```
